```python
import math
import jax, jax.numpy as jnp
from jax import lax
import numpy as np

D_MODEL = 1024
BATCH = 8
SEQ = 4096
DEPTH = 1
DEC_BATCH = 32
DEC_SEQ = 4
PAST_LEN = 16384
PAGE_SIZE = 128

MIX_WIDTH = D_MODEL
N_ATT_HEADS = 4
ATT_HEAD_DIM = MIX_WIDTH // (4 * N_ATT_HEADS)
ATT_V_DIM = 2 * ATT_HEAD_DIM
ATT_WIDTH = N_ATT_HEADS * ATT_V_DIM
QK_WIDTH = N_ATT_HEADS * 2 * ATT_HEAD_DIM
CONV_WIDTH = MIX_WIDTH - ATT_WIDTH
CONV_K = 3
PROJ_WIDTH = 2 * QK_WIDTH + ATT_WIDTH + 3 * CONV_WIDTH
NUM_BUCKETS = 32
MAX_DISTANCE = 128
PEER_HEADS = 8
PEER_N_KEYS = 128
PEER_N_EXPERTS = PEER_N_KEYS * PEER_N_KEYS
PEER_TOPK = 16
PEER_KEY_DIM = 256
PEER_HALF = PEER_KEY_DIM // 2
PEER_BLOCK = 128
QUERY_BLOCK = 128
LN_EPS = 1e-5
DEEPNORM_ALPHA = (2.0 * DEPTH) ** 0.25
DEEPNORM_BETA = (8.0 * DEPTH) ** -0.25

kernel_name = "hybrid_diffattn_shortconv_peer_decode_step"


def _layer_norm(x, g, b):
    xf = x.astype(jnp.float32)
    mu = jnp.mean(xf, axis=-1, keepdims=True)
    var = jnp.mean(jnp.square(xf - mu), axis=-1, keepdims=True)
    return ((xf - mu) * lax.rsqrt(var + LN_EPS) * g.astype(jnp.float32) + b.astype(jnp.float32)).astype(x.dtype)


def _adaln(c, w_ada, b_ada):
    m = (jax.nn.silu(c) @ w_ada + b_ada)[:, None, :]
    return jnp.split(m, 6, axis=-1)


def _t5_bucket(dist):
    n = jnp.maximum(dist, 0)
    max_exact = NUM_BUCKETS // 2
    n_large = jnp.maximum(n, max_exact).astype(jnp.float32)
    large = max_exact + (jnp.log(n_large / max_exact) / math.log(MAX_DISTANCE / max_exact)
                         * (NUM_BUCKETS - max_exact)).astype(jnp.int32)
    large = jnp.minimum(large, NUM_BUCKETS - 1)
    return jnp.where(n < max_exact, n, large)


def _diff_lambda(lq1, lk1, lq2, lk2, lam_init):
    f = jnp.float32
    return (jnp.exp(jnp.sum(lq1.astype(f) * lk1.astype(f)))
            - jnp.exp(jnp.sum(lq2.astype(f) * lk2.astype(f))) + lam_init)


def _diff_attention(q, k, v, q_pos, k_pos, rel_bias, lam):
    s = jnp.einsum('bqhmd,bkhmd->bmhqk', q, k,
                   preferred_element_type=jnp.float32) * (ATT_HEAD_DIM ** -0.5)
    dist = q_pos[:, None] - k_pos[None, :]
    bias = jnp.transpose(rel_bias.astype(jnp.float32)[_t5_bucket(dist)], (2, 0, 1))
    s = jnp.where(dist >= 0, s + bias, -jnp.inf)
    p = jax.nn.softmax(s, axis=-1)
    w = p[:, 0] - lam * p[:, 1]
    return jnp.einsum('bhqk,bkhd->bqhd', w.astype(v.dtype), v)


def _diff_post(o, subln_w, lam_init):
    of = o.astype(jnp.float32)
    of = of * lax.rsqrt(jnp.mean(jnp.square(of), axis=-1, keepdims=True) + LN_EPS)
    of = of * subln_w.astype(jnp.float32) * (1.0 - lam_init)
    return of.reshape(o.shape[0], o.shape[1], ATT_WIDTH).astype(o.dtype)


def _prompt_attend(q, k, v, lam, rel_bias):
    b, s = q.shape[0], q.shape[1]
    nqb = s // QUERY_BLOCK
    qb = jnp.moveaxis(q.reshape(b, nqb, QUERY_BLOCK, N_ATT_HEADS, 2, ATT_HEAD_DIM), 1, 0)
    k_pos = jnp.arange(s, dtype=jnp.int32)

    def block(args):
        qi, i = args
        q_pos = i * QUERY_BLOCK + jnp.arange(QUERY_BLOCK, dtype=jnp.int32)
        return _diff_attention(qi, k, v, q_pos, k_pos, rel_bias, lam)

    o = lax.map(block, (qb, jnp.arange(nqb, dtype=jnp.int32)))
    return jnp.moveaxis(o, 0, 1).reshape(b, s, N_ATT_HEADS, ATT_V_DIM)


def _sample_attend(q, k, v, lam, rel_bias, cache_k, cache_v, page_table):
    n_pages = page_table.shape[1]
    past = n_pages * cache_k.shape[1]
    t = q.shape[1]
    q_pos = past + jnp.arange(t, dtype=jnp.int32)
    k_pos = jnp.arange(past + t, dtype=jnp.int32)

    def one(args):
        pt, qi, ki, vi = args
        kp = cache_k[pt].reshape(past, N_ATT_HEADS, 2, ATT_HEAD_DIM)
        vp = cache_v[pt].reshape(past, N_ATT_HEADS, ATT_V_DIM)
        kf = jnp.concatenate([kp, ki.astype(kp.dtype)], axis=0)
        vf = jnp.concatenate([vp, vi.astype(vp.dtype)], axis=0)
        return _diff_attention(qi[None], kf[None], vf[None], q_pos, k_pos, rel_bias, lam)[0]

    return lax.map(one, (page_table, q, k, v))


def _short_conv(gb, gc, hc, prefix, conv_w):
    z = gc * hc
    zf = jnp.concatenate([prefix.astype(z.dtype), z], axis=1)
    s = z.shape[1]
    y = conv_w[0] * zf[:, 0:s] + conv_w[1] * zf[:, 1:s + 1] + conv_w[2] * zf[:, 2:s + 2]
    return gb * y, zf[:, -(CONV_K - 1):]


def _peer(h, w_pq, sub_keys, peer_u, peer_v):
    t, d = h.shape
    nb = -(-t // PEER_BLOCK)
    hp = jnp.pad(h, ((0, nb * PEER_BLOCK - t), (0, 0))).reshape(nb, PEER_BLOCK, d)

    def block(hb):
        q = (hb @ w_pq).reshape(PEER_BLOCK, PEER_HEADS, 2, PEER_HALF)
        s = jnp.einsum('thpd,hpnd->thpn', q, sub_keys, preferred_element_type=jnp.float32)
        sv, si = lax.top_k(s, PEER_TOPK)
        cand = (sv[:, :, 0, :, None] + sv[:, :, 1, None, :]).reshape(
            PEER_BLOCK, PEER_HEADS, PEER_TOPK * PEER_TOPK)
        cv, ci = lax.top_k(cand, PEER_TOPK)
        ea = jnp.take_along_axis(si[:, :, 0], ci // PEER_TOPK, axis=-1)
        eb = jnp.take_along_axis(si[:, :, 1], ci % PEER_TOPK, axis=-1)
        e = ea * PEER_N_KEYS + eb
        g = jax.nn.softmax(cv, axis=-1)
        a = jax.nn.gelu(jnp.einsum('thkd,td->thk', peer_u[e], hb,
                                   preferred_element_type=jnp.float32), approximate=False)
        return jnp.einsum('thk,thkd->td', (g * a).astype(peer_v.dtype), peer_v[e])

    return lax.map(block, hp).reshape(nb * PEER_BLOCK, d)[:t]


def _decoder_layer(x, c, conv_prefix, attend, layer_idx, p):
    lam_init = 0.8 - 0.6 * math.exp(-0.3 * layer_idx)
    b, s, d = x.shape
    shift1, scale1, gate1, shift2, scale2, gate2 = _adaln(c, p['w_ada'], p['b_ada'])
    h = x * (1 + scale1) + shift1
    proj = h @ p['w_in']
    o1 = 2 * QK_WIDTH
    o2 = o1 + ATT_WIDTH
    q = proj[..., :QK_WIDTH].reshape(b, s, N_ATT_HEADS, 2, ATT_HEAD_DIM)
    k = proj[..., QK_WIDTH:o1].reshape(b, s, N_ATT_HEADS, 2, ATT_HEAD_DIM)
    v = proj[..., o1:o2].reshape(b, s, N_ATT_HEADS, ATT_V_DIM)
    gb = proj[..., o2:o2 + CONV_WIDTH]
    gc = proj[..., o2 + CONV_WIDTH:o2 + 2 * CONV_WIDTH]
    hc = proj[..., o2 + 2 * CONV_WIDTH:]
    lam = _diff_lambda(p['lq1'], p['lk1'], p['lq2'], p['lk2'], lam_init)
    o_att = _diff_post(attend(q, k, v, lam), p['subln_w'], lam_init)
    o_conv, conv_state = _short_conv(gb, gc, hc, conv_prefix, p['conv_w'])
    mix = jnp.concatenate([o_att, o_conv], axis=-1) @ p['w_out']
    x = _layer_norm(DEEPNORM_ALPHA * x + gate1 * mix, p['ln1_g'], p['ln1_b'])
    h2 = x * (1 + scale2) + shift2
    f = _peer(h2.reshape(b * s, d), p['w_pq'], p['sub_keys'], p['peer_u'], p['peer_v']).reshape(b, s, d)
    x = _layer_norm(DEEPNORM_ALPHA * x + gate2 * f, p['ln2_g'], p['ln2_b'])
    return x, k, v, conv_state


def setup_inputs(seed: int = 0) -> dict:
    key = jax.random.key(seed)
    ks = jax.random.split(key, 32)
    f = jnp.float32
    D = D_MODEL
    n_pages = PAST_LEN // PAGE_SIZE
    n_used = DEC_BATCH * n_pages
    n_pool = n_used + (n_used + 3) // 4
    page_table = jax.random.permutation(ks[0], n_pool)[:n_used].reshape(DEC_BATCH, n_pages).astype(jnp.int32)
    nrm = lambda k, shape: jax.random.normal(k, shape, f)
    w_in = jnp.concatenate([
        nrm(ks[1], (DEPTH, D, 2 * QK_WIDTH)) * D ** -0.5,
        nrm(ks[2], (DEPTH, D, ATT_WIDTH)) * (D ** -0.5 * DEEPNORM_BETA),
        nrm(ks[3], (DEPTH, D, 3 * CONV_WIDTH)) * D ** -0.5], axis=-1)
    return {
        "x_prompt": nrm(ks[4], (BATCH, SEQ, D)),
        "x_sample": nrm(ks[5], (DEC_BATCH, DEC_SEQ, D)),
        "cache_k": nrm(ks[6], (DEPTH, n_pool, PAGE_SIZE, N_ATT_HEADS, 2, ATT_HEAD_DIM)),
        "cache_v": nrm(ks[7], (DEPTH, n_pool, PAGE_SIZE, N_ATT_HEADS, ATT_V_DIM)),
        "state_conv": nrm(ks[8], (DEPTH, DEC_BATCH, CONV_K - 1, CONV_WIDTH)),
        "page_table": page_table,
        "c_prompt": nrm(ks[9], (BATCH, D)),
        "c_sample": nrm(ks[10], (DEC_BATCH, D)),
        "rel_bias": nrm(ks[11], (NUM_BUCKETS, N_ATT_HEADS)) * 0.5,
        "w_ada": nrm(ks[12], (DEPTH, D, 6 * D)) * D ** -0.5,
        "b_ada": nrm(ks[13], (DEPTH, 6 * D)) * 0.01,
        "w_in": w_in,
        "lambda_q1": nrm(ks[14], (DEPTH, ATT_HEAD_DIM)) * 0.1,
        "lambda_k1": nrm(ks[15], (DEPTH, ATT_HEAD_DIM)) * 0.1,
        "lambda_q2": nrm(ks[16], (DEPTH, ATT_HEAD_DIM)) * 0.1,
        "lambda_k2": nrm(ks[17], (DEPTH, ATT_HEAD_DIM)) * 0.1,
        "subln_w": 1.0 + 0.01 * nrm(ks[18], (DEPTH, ATT_V_DIM)),
        "conv_w": nrm(ks[19], (DEPTH, CONV_K, CONV_WIDTH)) * CONV_K ** -0.5,
        "w_out": nrm(ks[20], (DEPTH, MIX_WIDTH, D)) * (MIX_WIDTH ** -0.5 * DEEPNORM_BETA),
        "ln1_g": 1.0 + 0.01 * nrm(ks[21], (DEPTH, D)),
        "ln1_b": 0.01 * nrm(ks[22], (DEPTH, D)),
        "w_pq": nrm(ks[23], (DEPTH, D, PEER_HEADS * PEER_KEY_DIM)) * D ** -0.5,
        "sub_keys": nrm(ks[24], (DEPTH, PEER_HEADS, 2, PEER_N_KEYS, PEER_HALF)) * PEER_HALF ** -0.5,
        "peer_u": nrm(ks[25], (DEPTH, PEER_N_EXPERTS, D)) * (D ** -0.5 * DEEPNORM_BETA),
        "peer_v": nrm(ks[26], (DEPTH, PEER_N_EXPERTS, D)) * DEEPNORM_BETA,
        "ln2_g": 1.0 + 0.01 * nrm(ks[27], (DEPTH, D)),
        "ln2_b": 0.01 * nrm(ks[28], (DEPTH, D)),
    }


def reference(x_prompt, x_sample, cache_k, cache_v, state_conv, page_table, c_prompt, c_sample,
              rel_bias, w_ada, b_ada, w_in, lambda_q1, lambda_k1, lambda_q2, lambda_k2, subln_w,
              conv_w, w_out, ln1_g, ln1_b, w_pq, sub_keys, peer_u, peer_v, ln2_g, ln2_b):
    y_prompt, y_sample = x_prompt, x_sample
    kp_list, vp_list, cp_list, ks_list, vs_list, cs_list = [], [], [], [], [], []
    zero_prefix = jnp.zeros((x_prompt.shape[0], CONV_K - 1, CONV_WIDTH), x_prompt.dtype)
    for layer in range(DEPTH):
        p = dict(w_ada=w_ada[layer], b_ada=b_ada[layer], w_in=w_in[layer],
                 lq1=lambda_q1[layer], lk1=lambda_k1[layer], lq2=lambda_q2[layer], lk2=lambda_k2[layer],
                 subln_w=subln_w[layer], conv_w=conv_w[layer], w_out=w_out[layer],
                 ln1_g=ln1_g[layer], ln1_b=ln1_b[layer], w_pq=w_pq[layer], sub_keys=sub_keys[layer],
                 peer_u=peer_u[layer], peer_v=peer_v[layer], ln2_g=ln2_g[layer], ln2_b=ln2_b[layer])
        ck, cv = cache_k[layer], cache_v[layer]
        attend_prompt = lambda q, k, v, lam: _prompt_attend(q, k, v, lam, rel_bias)
        attend_sample = lambda q, k, v, lam: _sample_attend(q, k, v, lam, rel_bias, ck, cv, page_table)
        y_prompt, kp, vp, cp = _decoder_layer(y_prompt, c_prompt, zero_prefix, attend_prompt, layer, p)
        y_sample, ksn, vsn, csn = _decoder_layer(y_sample, c_sample, state_conv[layer], attend_sample, layer, p)
        kp_list.append(kp)
        vp_list.append(vp)
        cp_list.append(cp)
        ks_list.append(ksn)
        vs_list.append(vsn)
        cs_list.append(csn)
    new_k_prompt = jnp.stack(kp_list)
    new_v_prompt = jnp.stack(vp_list)
    new_conv_prompt = jnp.stack(cp_list)
    new_k_sample = jnp.stack(ks_list)
    new_v_sample = jnp.stack(vs_list)
    new_conv_sample = jnp.stack(cs_list)
    return (y_prompt, y_sample, new_k_prompt, new_v_prompt, new_conv_prompt, new_k_sample, new_v_sample, new_conv_sample)
```

```python
import functools
import math

import numpy as np
import jax
import jax.numpy as jnp
from jax import lax
from jax.experimental import pallas as pl
from jax.experimental.pallas import tpu as pltpu

N_HEADS = 4
HEAD_DIM = 64
V_DIM = 2 * HEAD_DIM
ATT_WIDTH = N_HEADS * V_DIM
QK_WIDTH = N_HEADS * 2 * HEAD_DIM
CONV_K = 3
NUM_BUCKETS = 32
MAX_DISTANCE = 128
PEER_HEADS = 8
PEER_N_KEYS = 128
PEER_TOPK = 16
LN_EPS = 1e-5

LANES = 128
SUBLANES = 8
VMEM_LIMIT_BYTES = 56 * 1024 * 1024

NEG = -1e30
BF16 = jnp.bfloat16
F32 = jnp.float32


def _cparams(sem):
    return pltpu.CompilerParams(dimension_semantics=sem, vmem_limit_bytes=VMEM_LIMIT_BYTES)


def _dot(a, b):
    return jnp.dot(a, b, preferred_element_type=F32)


def _dot_nt(a, b):
    return lax.dot_general(a, b, (((1,), (1,)), ((), ())), preferred_element_type=F32)


def _adaln_kernel(c_ref, w_ref, b_ref, o_ref):
    c = c_ref[...]
    s = c * jax.nn.sigmoid(c)
    o_ref[...] = _dot(s.astype(BF16), w_ref[...].astype(BF16)) + b_ref[...]


def _adaln(c, w_ada, b_ada):
    n, d = c.shape
    width = w_ada.shape[1]
    tn = width // 4
    return pl.pallas_call(
        _adaln_kernel,
        grid=(width // tn,),
        in_specs=[pl.BlockSpec((n, d), lambda j: (0, 0)),
                  pl.BlockSpec((d, tn), lambda j: (0, j)),
                  pl.BlockSpec((1, tn), lambda j: (0, j))],
        out_specs=pl.BlockSpec((n, tn), lambda j: (0, j)),
        out_shape=jax.ShapeDtypeStruct((n, width), F32),
        compiler_params=_cparams(("arbitrary",)),
        name="adaln",
    )(c, w_ada, b_ada.reshape(1, width))


def _inproj_kernel(x_ref, mod_ref, pre_ref, w_ref, cw_ref, q_ref, k_ref, v_ref, oc_ref, cs_ref, zbuf,
                   *, ts, d, cw):
    @pl.when(pl.program_id(1) == 0)
    def _():
        zbuf[SUBLANES - 2:SUBLANES, :] = pre_ref[0]

    shift1 = mod_ref[0, :, 0:d]
    scale1 = mod_ref[0, :, d:2 * d]
    h = x_ref[0] * (1.0 + scale1) + shift1
    proj = _dot(h.astype(BF16), w_ref[...])
    o1 = 2 * QK_WIDTH
    o2 = o1 + ATT_WIDTH
    q_ref[0] = proj[:, :QK_WIDTH]
    k_ref[0] = proj[:, QK_WIDTH:o1]
    v_ref[0] = proj[:, o1:o2]
    gb = proj[:, o2:o2 + cw]
    z = proj[:, o2 + cw:o2 + 2 * cw] * proj[:, o2 + 2 * cw:o2 + 3 * cw]
    zbuf[SUBLANES:SUBLANES + ts, :] = z
    y = (cw_ref[0:1, :] * zbuf[SUBLANES - 2:SUBLANES - 2 + ts, :]
         + cw_ref[1:2, :] * zbuf[SUBLANES - 1:SUBLANES - 1 + ts, :]
         + cw_ref[2:3, :] * z)
    oc_ref[0] = gb * y
    tail = zbuf[SUBLANES + ts - 2:SUBLANES + ts, :]
    cs_ref[0] = tail
    zbuf[SUBLANES - 2:SUBLANES, :] = tail


def _inproj(x, mod, prefix, w_in_bf, conv_w, ts):
    b, s, d = x.shape
    cw = conv_w.shape[1]
    pw = w_in_bf.shape[1]
    blk = lambda width: pl.BlockSpec((1, ts, width), lambda i, j: (i, j, 0))
    outs = pl.pallas_call(
        functools.partial(_inproj_kernel, ts=ts, d=d, cw=cw),
        grid=(b, s // ts),
        in_specs=[blk(d),
                  pl.BlockSpec((1, 1, mod.shape[2]), lambda i, j: (i, 0, 0)),
                  pl.BlockSpec((1, CONV_K - 1, cw), lambda i, j: (i, 0, 0)),
                  pl.BlockSpec((d, pw), lambda i, j: (0, 0)),
                  pl.BlockSpec((CONV_K, cw), lambda i, j: (0, 0))],
        out_specs=[blk(QK_WIDTH), blk(QK_WIDTH), blk(ATT_WIDTH), blk(cw),
                   pl.BlockSpec((1, CONV_K - 1, cw), lambda i, j: (i, 0, 0))],
        out_shape=[jax.ShapeDtypeStruct((b, s, QK_WIDTH), F32),
                   jax.ShapeDtypeStruct((b, s, QK_WIDTH), F32),
                   jax.ShapeDtypeStruct((b, s, ATT_WIDTH), F32),
                   jax.ShapeDtypeStruct((b, s, cw), F32),
                   jax.ShapeDtypeStruct((b, CONV_K - 1, cw), F32)],
        scratch_shapes=[pltpu.VMEM((SUBLANES + ts, cw), F32)],
        compiler_params=_cparams(("arbitrary", "arbitrary")),
        name="inproj",
    )(x, mod, prefix, w_in_bf, conv_w)
    return outs


def _t5_bucket_np(dist):
    dist = np.asarray(dist, np.int64)
    n = np.maximum(dist, 0)
    max_exact = NUM_BUCKETS // 2
    n_large = np.maximum(n, max_exact).astype(np.float64)
    large = max_exact + (np.log(n_large / max_exact) / math.log(MAX_DISTANCE / max_exact)
                         * (NUM_BUCKETS - max_exact)).astype(np.int64)
    large = np.minimum(large, NUM_BUCKETS - 1)
    bucket = np.where(n < max_exact, n, large)
    return np.where(dist < 0, -1, bucket).astype(np.int32)


def _bias_kernel(rb_ref, bk_ref, o_ref):
    h = pl.program_id(0)
    bk = bk_ref[0]
    acc = jnp.where(bk < 0, NEG, 0.0).astype(F32)
    for b in range(NUM_BUCKETS):
        acc = jnp.where(bk == b, rb_ref[b, h], acc)
    o_ref[0, 0] = acc


def _bias_tiles(rel_bias, buckets):
    n, r, c = buckets.shape
    return pl.pallas_call(
        _bias_kernel,
        grid=(N_HEADS, n),
        in_specs=[pl.BlockSpec(memory_space=pltpu.SMEM),
                  pl.BlockSpec((1, r, c), lambda h, i: (i, 0, 0))],
        out_specs=pl.BlockSpec((1, 1, r, c), lambda h, i: (h, i, 0, 0)),
        out_shape=jax.ShapeDtypeStruct((N_HEADS, n, r, c), F32),
        compiler_params=_cparams(("arbitrary", "arbitrary")),
        name="bias_tiles",
    )(rel_bias, jnp.asarray(buckets))


def _far_bucket_from(dist_lo):
    b = _t5_bucket_np(np.arange(dist_lo, dist_lo + 4 * MAX_DISTANCE))
    assert (b == NUM_BUCKETS - 1).all()
    return NUM_BUCKETS - 1


def _diff_lambda(lq1_ref, lk1_ref, lq2_ref, lk2_ref, lam_init):
    a = jnp.sum(lq1_ref[...] * lk1_ref[...], axis=1, keepdims=True)
    b = jnp.sum(lq2_ref[...] * lk2_ref[...], axis=1, keepdims=True)
    return jnp.exp(a) - jnp.exp(b) + lam_init


def _split_q(q):
    lane = lax.broadcasted_iota(jnp.int32, q.shape, 1)
    q1 = jnp.where(lane < HEAD_DIM, q, 0.0)
    q2 = jnp.where(lane >= HEAD_DIM, q, 0.0)
    return jnp.concatenate([q1, q2], axis=0).astype(BF16)


def _online_update(s, v_bf, m_ref, l_ref, acc_ref, rows):
    m_old = m_ref[rows]
    m_new = jnp.maximum(m_old, jnp.max(s, axis=1, keepdims=True))
    p = jnp.exp(s - m_new)
    alpha = jnp.exp(m_old - m_new)
    l_ref[rows] = alpha * l_ref[rows] + jnp.sum(p, axis=1, keepdims=True)
    acc_ref[rows] = alpha * acc_ref[rows] + _dot(p.astype(BF16), v_bf)
    m_ref[rows] = m_new


def _diff_finish(acc, l, lam, sw, nq, lam_init):
    o = acc[:nq] / l[:nq] - lam * (acc[nq:] / l[nq:])
    o = o * lax.rsqrt(jnp.mean(o * o, axis=1, keepdims=True) + LN_EPS)
    return o * sw * (1.0 - lam_init)


def _pattn_kernel(rb_ref, q_ref, k_ref, v_ref, bt_ref, lq1, lk1, lq2, lk2, sw_ref, o_ref,
                  m_ref, l_ref, acc_ref, *, qb, far_bucket, lam_init):
    h = pl.program_id(1)
    i = pl.program_id(2)
    q2 = _split_q(q_ref[0] * (HEAD_DIM ** -0.5))
    m_ref[...] = jnp.full(m_ref.shape, NEG, F32)
    l_ref[...] = jnp.zeros(l_ref.shape, F32)
    acc_ref[...] = jnp.zeros(acc_ref.shape, F32)
    c_far = rb_ref[far_bucket, h]
    rows = slice(None)

    def tile(j, bias):
        off = pl.multiple_of(j * qb, qb)
        k = k_ref[0, pl.ds(off, qb), :].astype(BF16)
        v = v_ref[0, pl.ds(off, qb), :].astype(BF16)
        s = _dot_nt(q2, k) + bias
        _online_update(s, v, m_ref, l_ref, acc_ref, rows)

    def far(j, carry):
        tile(j, c_far)
        return carry

    lax.fori_loop(0, jnp.maximum(i - 1, 0), far, 0)

    @pl.when(i >= 1)
    def _():
        tile(i - 1, bt_ref[0, 1])

    tile(i, bt_ref[0, 0])
    lam = _diff_lambda(lq1, lk1, lq2, lk2, lam_init)
    o_ref[0] = _diff_finish(acc_ref[...], l_ref[...], lam, sw_ref[...], qb, lam_init)


def _prompt_attention(q, k, v, rel_bias, lq1, lk1, lq2, lk2, subln_w, lam_init, qb):
    b, s, _ = q.shape
    ii = np.arange(qb)[:, None]
    jj = np.arange(qb)[None, :]
    d0 = _t5_bucket_np(ii - jj)
    d1 = _t5_bucket_np(qb + ii - jj)
    buckets = np.stack([np.concatenate([d0, d0], 0), np.concatenate([d1, d1], 0)])
    far_bucket = _far_bucket_from(qb + 1)
    bt = _bias_tiles(rel_bias, buckets)
    vec = lambda n: pl.BlockSpec((1, n), lambda bi, h, i: (0, 0))
    return pl.pallas_call(
        functools.partial(_pattn_kernel, qb=qb, far_bucket=far_bucket, lam_init=lam_init),
        grid=(b, N_HEADS, s // qb),
        in_specs=[pl.BlockSpec(memory_space=pltpu.SMEM),
                  pl.BlockSpec((1, qb, V_DIM), lambda bi, h, i: (bi, i, h)),
                  pl.BlockSpec((1, s, V_DIM), lambda bi, h, i: (bi, 0, h)),
                  pl.BlockSpec((1, s, V_DIM), lambda bi, h, i: (bi, 0, h)),
                  pl.BlockSpec((1, 2, 2 * qb, qb), lambda bi, h, i: (h, 0, 0, 0)),
                  vec(HEAD_DIM), vec(HEAD_DIM), vec(HEAD_DIM), vec(HEAD_DIM), vec(V_DIM)],
        out_specs=pl.BlockSpec((1, qb, V_DIM), lambda bi, h, i: (bi, i, h)),
        out_shape=jax.ShapeDtypeStruct((b, s, ATT_WIDTH), F32),
        scratch_shapes=[pltpu.VMEM((2 * qb, 1), F32), pltpu.VMEM((2 * qb, 1), F32),
                        pltpu.VMEM((2 * qb, V_DIM), F32)],
        compiler_params=_cparams(("arbitrary", "arbitrary", "arbitrary")),
        name="prompt_attention",
    )(rel_bias, q, k, v, bt, lq1, lk1, lq2, lk2, subln_w)


def _sattn_kernel(pt_ref, q_ref, kn_ref, vn_ref, bt_ref, bn_ref, lq1, lk1, lq2, lk2, sw_ref, *rest,
                  pps, t, lam_init):
    kp = rest[:pps]
    vp = rest[pps:2 * pps]
    o_ref = rest[2 * pps]
    m_ref, l_ref, acc_ref = rest[2 * pps + 1:]
    j = pl.program_id(1)
    nj = pl.num_programs(1)
    nq = 2 * t

    @pl.when(j == 0)
    def _():
        m_ref[...] = jnp.full(m_ref.shape, NEG, F32)
        l_ref[...] = jnp.zeros(l_ref.shape, F32)
        acc_ref[...] = jnp.zeros(acc_ref.shape, F32)

    q = q_ref[0] * (HEAD_DIM ** -0.5)
    for h in range(N_HEADS):
        cols = slice(h * V_DIM, (h + 1) * V_DIM)
        rows = slice(h * nq, (h + 1) * nq)
        q2 = _split_q(q[:, cols])
        s = jnp.concatenate([_dot_nt(q2, kp[r][0, :, cols].astype(BF16)) for r in range(pps)], axis=1)
        s = s + bt_ref[h, 0]
        m_old = m_ref[rows]
        m_new = jnp.maximum(m_old, jnp.max(s, axis=1, keepdims=True))
        p = jnp.exp(s - m_new)
        alpha = jnp.exp(m_old - m_new)
        l_ref[rows] = alpha * l_ref[rows] + jnp.sum(p, axis=1, keepdims=True)
        p = p.astype(BF16)
        pv = _dot(p[:, 0:LANES], vp[0][0, :, cols].astype(BF16))
        for r in range(1, pps):
            pv = pv + _dot(p[:, r * LANES:(r + 1) * LANES], vp[r][0, :, cols].astype(BF16))
        acc_ref[rows] = alpha * acc_ref[rows] + pv
        m_ref[rows] = m_new

    @pl.when(j == nj - 1)
    def _():
        lam = _diff_lambda(lq1, lk1, lq2, lk2, lam_init)
        for h in range(N_HEADS):
            cols = slice(h * V_DIM, (h + 1) * V_DIM)
            rows = slice(h * nq, (h + 1) * nq)
            q2 = _split_q(q[:, cols])
            s = _dot_nt(q2, kn_ref[0, :, cols].astype(BF16)) + bn_ref[h, 0]
            _online_update(s, vn_ref[0, :, cols].astype(BF16), m_ref, l_ref, acc_ref, rows)
            o_ref[0, :, cols] = _diff_finish(acc_ref[rows], l_ref[rows], lam, sw_ref[...], t, lam_init)


def _sample_attention(q, k_new, v_new, cache_k, cache_v, page_table, rel_bias,
                      lq1, lk1, lq2, lk2, subln_w, lam_init, pps):
    bs, t, _ = q.shape
    n_pool, page = cache_k.shape[0], cache_k.shape[1]
    n_pages = page_table.shape[1]
    past = n_pages * page
    ck = cache_k.reshape(n_pool, page, QK_WIDTH)
    cv = cache_v.reshape(n_pool, page, ATT_WIDTH)
    nj = n_pages // pps
    chunk = pps * page
    tq = np.tile(np.arange(t), 2)[:, None]
    kc = np.arange(chunk)[None, :]
    last = _t5_bucket_np(past + tq - (past - chunk + kc))
    assert (_t5_bucket_np(past - (past - chunk) + 1 + np.arange(4 * MAX_DISTANCE)) == NUM_BUCKETS - 1).all()
    far = np.full_like(last, NUM_BUCKETS - 1)
    kn = np.arange(page)[None, :]
    newb = np.where(kn < t, _t5_bucket_np(tq - kn), -1).astype(np.int32)
    bt = _bias_tiles(rel_bias, np.stack([far, last]))
    bn = _bias_tiles(rel_bias, newb[None])
    knp = jnp.pad(k_new, ((0, 0), (0, page - t), (0, 0)))
    vnp = jnp.pad(v_new, ((0, 0), (0, page - t), (0, 0)))
    vec = lambda n: pl.BlockSpec((1, n), lambda b, j, pt: (0, 0))

    def page_spec(r):
        return pl.BlockSpec((1, page, QK_WIDTH), lambda b, j, pt: (pt[b, j * pps + r], 0, 0))

    grid_spec = pltpu.PrefetchScalarGridSpec(
        num_scalar_prefetch=1,
        grid=(bs, nj),
        in_specs=[pl.BlockSpec((1, t, QK_WIDTH), lambda b, j, pt: (b, 0, 0)),
                  pl.BlockSpec((1, page, QK_WIDTH), lambda b, j, pt: (b, 0, 0)),
                  pl.BlockSpec((1, page, ATT_WIDTH), lambda b, j, pt: (b, 0, 0)),
                  pl.BlockSpec((N_HEADS, 1, 2 * t, chunk), lambda b, j, pt: (0, (j == nj - 1).astype(jnp.int32), 0, 0)),
                  pl.BlockSpec((N_HEADS, 1, 2 * t, page), lambda b, j, pt: (0, 0, 0, 0)),
                  vec(HEAD_DIM), vec(HEAD_DIM), vec(HEAD_DIM), vec(HEAD_DIM), vec(V_DIM)]
                 + [page_spec(r) for r in range(pps)] + [page_spec(r) for r in range(pps)],
        out_specs=pl.BlockSpec((1, t, ATT_WIDTH), lambda b, j, pt: (b, 0, 0)),
        scratch_shapes=[pltpu.VMEM((N_HEADS * 2 * t, 1), F32), pltpu.VMEM((N_HEADS * 2 * t, 1), F32),
                        pltpu.VMEM((N_HEADS * 2 * t, V_DIM), F32)],
    )
    return pl.pallas_call(
        functools.partial(_sattn_kernel, pps=pps, t=t, lam_init=lam_init),
        grid_spec=grid_spec,
        out_shape=jax.ShapeDtypeStruct((bs, t, ATT_WIDTH), F32),
        compiler_params=_cparams(("arbitrary", "arbitrary")),
        name="sample_attention",
    )(page_table, q, knp, vnp, bt, bn, lq1, lk1, lq2, lk2, subln_w, *([ck] * pps), *([cv] * pps))


def _layer_norm(y, g, b):
    mu = jnp.mean(y, axis=1, keepdims=True)
    yc = y - mu
    var = jnp.mean(yc * yc, axis=1, keepdims=True)
    return yc * lax.rsqrt(var + LN_EPS) * g + b


def _outproj_kernel(oa_ref, oc_ref, x_ref, mod_ref, w_ref, g_ref, b_ref, x1_ref, h2_ref, *, d, alpha):
    aw = oa_ref.shape[2]
    mix = _dot(oa_ref[0].astype(BF16), w_ref[0:aw, :]) + _dot(oc_ref[0].astype(BF16), w_ref[aw:, :])
    gate1 = mod_ref[0, :, 2 * d:3 * d]
    shift2 = mod_ref[0, :, 3 * d:4 * d]
    scale2 = mod_ref[0, :, 4 * d:5 * d]
    x1 = _layer_norm(alpha * x_ref[0] + gate1 * mix, g_ref[...], b_ref[...])
    x1_ref[0] = x1
    h2_ref[0] = x1 * (1.0 + scale2) + shift2


def _outproj(oa, oc, x, mod, w_out_bf, g, bb, alpha, ts):
    b, s, d = x.shape
    blk = lambda width: pl.BlockSpec((1, ts, width), lambda i, j: (i, j, 0))
    vec = pl.BlockSpec((1, d), lambda i, j: (0, 0))
    return pl.pallas_call(
        functools.partial(_outproj_kernel, d=d, alpha=alpha),
        grid=(b, s // ts),
        in_specs=[blk(oa.shape[2]), blk(oc.shape[2]), blk(d),
                  pl.BlockSpec((1, 1, mod.shape[2]), lambda i, j: (i, 0, 0)),
                  pl.BlockSpec(w_out_bf.shape, lambda i, j: (0, 0)), vec, vec],
        out_specs=[blk(d), blk(d)],
        out_shape=[jax.ShapeDtypeStruct((b, s, d), F32), jax.ShapeDtypeStruct((b, s, d), F32)],
        compiler_params=_cparams(("arbitrary", "arbitrary")),
        name="outproj_ln1",
    )(oa, oc, x, mod, w_out_bf, g, bb)


_CAND_BLOCKS = [(i, 16 if i == 0 else 8, PEER_TOPK // (i + 1)) for i in range(8)]


def _top16_rows(x, iota_n, n, v_ref, i_ref):
    for r in range(PEER_TOPK):
        m = jnp.max(x, axis=0, keepdims=True)
        idx = jnp.min(jnp.where(x == m, iota_n, n), axis=0, keepdims=True)
        v_ref[r:r + 1, :] = m
        i_ref[r:r + 1, :] = idx
        x = jnp.where(iota_n == idx, -jnp.inf, x)


def _route_kernel(h_ref, w_ref, sk_ref, e_ref, g_ref, qt_ref, sv_ref, si_ref, cv_ref, ce_ref, et_ref, gt_ref,
                  *, tb):
    nlb = tb // LANES
    hb = h_ref[...].astype(BF16)
    qt_ref[...] = _dot_nt(w_ref[...], hb)
    iota_n = lax.broadcasted_iota(jnp.int32, (PEER_N_KEYS, LANES), 0)
    half = sk_ref.shape[2]

    def stage1(hp, carry):
        row0 = pl.multiple_of(hp * half, half)
        st = _dot(sk_ref[hp].astype(BF16), qt_ref[pl.ds(row0, half), :].astype(BF16))
        for lb in range(nlb):
            _top16_rows(st[:, lb * LANES:(lb + 1) * LANES], iota_n, PEER_N_KEYS,
                        sv_ref.at[hp, :, lb * LANES:(lb + 1) * LANES],
                        si_ref.at[hp, :, lb * LANES:(lb + 1) * LANES])
        return carry

    lax.fori_loop(0, 2 * PEER_HEADS, stage1, 0)

    sub8 = lax.broadcasted_iota(jnp.int32, (8, LANES), 0)
    sub16 = lax.broadcasted_iota(jnp.int32, (16, LANES), 0)

    def stage2(h, carry):
        for lb in range(nlb):
            ls = slice(lb * LANES, (lb + 1) * LANES)
            sv0 = sv_ref[2 * h, :, ls]
            sv1 = sv_ref[2 * h + 1, :, ls]
            si0 = si_ref[2 * h, :, ls]
            si1 = si_ref[2 * h + 1, :, ls]
            vals, flats, es = [], [], []
            for i, rows, cnt in _CAND_BLOCKS:
                sub = sub16 if rows == 16 else sub8
                vals.append(jnp.where(sub < cnt, sv0[i:i + 1] + sv1[0:rows], -jnp.inf))
                flats.append(i * PEER_TOPK + sub)
                es.append(si0[i:i + 1] * PEER_N_KEYS + si1[0:rows])
            vals.append(sv0[8:16] + sv1[0:1])
            flats.append((sub8 + 8) * PEER_TOPK)
            es.append(si0[8:16] * PEER_N_KEYS + si1[0:1])
            cand = jnp.concatenate(vals, axis=0)
            flat = jnp.concatenate(flats, axis=0)
            ecand = jnp.concatenate(es, axis=0)
            big = PEER_TOPK * PEER_TOPK
            for r in range(PEER_TOPK):
                m = jnp.max(cand, axis=0, keepdims=True)
                fl = jnp.min(jnp.where(cand == m, flat, big), axis=0, keepdims=True)
                hit = flat == fl
                cv_ref[r:r + 1, :] = m
                ce_ref[r:r + 1, :] = jnp.max(jnp.where(hit, ecand, -1), axis=0, keepdims=True)
                cand = jnp.where(hit, -jnp.inf, cand)
            cv = cv_ref[...]
            ex = jnp.exp(cv - cv[0:1])
            g = ex / jnp.sum(ex, axis=0, keepdims=True)
            row0 = pl.multiple_of(h * PEER_TOPK, PEER_TOPK)
            gt_ref[pl.ds(row0, PEER_TOPK), ls] = g
            et_ref[pl.ds(row0, PEER_TOPK), ls] = ce_ref[...]
        return carry

    lax.fori_loop(0, PEER_HEADS, stage2, 0)
    e_ref[...] = et_ref[...].T
    g_ref[...] = gt_ref[...].T


def _peer_route(h2, w_pq_t_bf, sub_keys, tb):
    t, d = h2.shape
    qw = w_pq_t_bf.shape[0]
    hp, n_keys, half = sub_keys.shape
    slots = PEER_HEADS * PEER_TOPK
    return pl.pallas_call(
        functools.partial(_route_kernel, tb=tb),
        grid=(t // tb,),
        in_specs=[pl.BlockSpec((tb, d), lambda i: (i, 0)),
                  pl.BlockSpec((qw, d), lambda i: (0, 0)),
                  pl.BlockSpec((hp, n_keys, half), lambda i: (0, 0, 0))],
        out_specs=[pl.BlockSpec((tb, slots), lambda i: (i, 0)), pl.BlockSpec((tb, slots), lambda i: (i, 0))],
        out_shape=[jax.ShapeDtypeStruct((t, slots), jnp.int32), jax.ShapeDtypeStruct((t, slots), F32)],
        scratch_shapes=[pltpu.VMEM((qw, tb), F32),
                        pltpu.VMEM((hp, PEER_TOPK, tb), F32), pltpu.VMEM((hp, PEER_TOPK, tb), jnp.int32),
                        pltpu.VMEM((PEER_TOPK, LANES), F32), pltpu.VMEM((PEER_TOPK, LANES), jnp.int32),
                        pltpu.VMEM((slots, tb), jnp.int32), pltpu.VMEM((slots, tb), F32)],
        compiler_params=_cparams(("arbitrary",)),
        name="peer_route",
    )(h2, w_pq_t_bf, sub_keys)


def _expand_kernel(e_ref, g_ref, w_ref, *, tb):
    iota = lax.broadcasted_iota(jnp.int32, (PEER_N_KEYS, e_ref.shape[1]), 0)

    def body(t, carry):
        er = e_ref[pl.ds(t, 1), :]
        gr = g_ref[pl.ds(t, 1), :]
        ea = er // PEER_N_KEYS
        eb = er - ea * PEER_N_KEYS
        oa = jnp.where(iota == ea, gr, 0.0).astype(BF16)
        ob = jnp.where(iota == eb, 1.0, 0.0).astype(BF16)
        w_ref[t] = _dot_nt(oa, ob)
        return carry

    lax.fori_loop(0, tb, body, 0)


def _peer_expand(e, g, tb):
    t, slots = e.shape
    return pl.pallas_call(
        functools.partial(_expand_kernel, tb=tb),
        grid=(t // tb,),
        in_specs=[pl.BlockSpec((tb, slots), lambda i: (i, 0)), pl.BlockSpec((tb, slots), lambda i: (i, 0))],
        out_specs=pl.BlockSpec((tb, PEER_N_KEYS, PEER_N_KEYS), lambda i: (i, 0, 0)),
        out_shape=jax.ShapeDtypeStruct((t, PEER_N_KEYS, PEER_N_KEYS), F32),
        compiler_params=_cparams(("arbitrary",)),
        name="peer_expand",
    )(e, g)


def _dense_kernel(h_ref, ut_ref, v_ref, w_ref, f_ref, hb_ref, *, eb, ec):
    j = pl.program_id(1)

    @pl.when(j == 0)
    def _():
        hb_ref[...] = h_ref[...].astype(BF16)
        f_ref[...] = jnp.zeros(f_ref.shape, F32)

    hb = hb_ref[...]
    for c in range(eb // ec):
        a = _dot(hb, ut_ref[:, c * ec:(c + 1) * ec])
        a = 0.5 * a * (1.0 + lax.erf(a * (2.0 ** -0.5)))
        parts = []
        for al in range(ec // PEER_N_KEYS):
            ai = c * (ec // PEER_N_KEYS) + al
            parts.append((a[:, al * PEER_N_KEYS:(al + 1) * PEER_N_KEYS] * w_ref[:, ai, :]).astype(BF16))
        wa = jnp.concatenate(parts, axis=1)
        f_ref[...] += _dot(wa, v_ref[c * ec:(c + 1) * ec, :])


def _peer_dense(h2, u_t_bf, v_bf, w, tb, eb, ec):
    t, d = h2.shape
    n_exp = v_bf.shape[0]
    return pl.pallas_call(
        functools.partial(_dense_kernel, eb=eb, ec=ec),
        grid=(t // tb, n_exp // eb),
        in_specs=[pl.BlockSpec((tb, d), lambda i, j: (i, 0)),
                  pl.BlockSpec((d, eb), lambda i, j: (0, j)),
                  pl.BlockSpec((eb, d), lambda i, j: (j, 0)),
                  pl.BlockSpec((tb, eb // PEER_N_KEYS, PEER_N_KEYS), lambda i, j: (i, j, 0))],
        out_specs=pl.BlockSpec((tb, d), lambda i, j: (i, 0)),
        out_shape=jax.ShapeDtypeStruct((t, d), F32),
        scratch_shapes=[pltpu.VMEM((tb, d), BF16)],
        compiler_params=_cparams(("arbitrary", "arbitrary")),
        name="peer_dense",
    )(h2, u_t_bf, v_bf, w)


def _final_kernel(x1_ref, f_ref, mod_ref, g_ref, b_ref, y_ref, *, d, alpha):
    gate2 = mod_ref[0, :, 5 * d:6 * d]
    y_ref[0] = _layer_norm(alpha * x1_ref[0] + gate2 * f_ref[0], g_ref[...], b_ref[...])


def _final_ln(x1, f, mod, g, bb, alpha, ts):
    b, s, d = x1.shape
    blk = pl.BlockSpec((1, ts, d), lambda i, j: (i, j, 0))
    vec = pl.BlockSpec((1, d), lambda i, j: (0, 0))
    return pl.pallas_call(
        functools.partial(_final_kernel, d=d, alpha=alpha),
        grid=(b, s // ts),
        in_specs=[blk, blk, pl.BlockSpec((1, 1, mod.shape[2]), lambda i, j: (i, 0, 0)), vec, vec],
        out_specs=blk,
        out_shape=jax.ShapeDtypeStruct((b, s, d), F32),
        compiler_params=_cparams(("arbitrary", "arbitrary")),
        name="final_ln2",
    )(x1, f, mod, g, bb)


def _pick(n, prefs):
    for p in prefs:
        if n % p == 0:
            return p
    return n


def _decoder_layer(x, mod, prefix, attend, p, alpha):
    b, s, d = x.shape
    ts = _pick(s, (512, 256, 128))
    q, k, v, oc, conv_state = _inproj(x, mod, prefix, p["w_in"], p["conv_w"], ts)
    oa = attend(q, k, v)
    x1, h2 = _outproj(oa, oc, x, mod, p["w_out"], p["ln1_g"], p["ln1_b"], alpha, ts)
    t = b * s
    h2f = h2.reshape(t, d)
    e, g = _peer_route(h2f, p["w_pq_t"], p["sub_keys"], _pick(t, (256, 128)))
    w = _peer_expand(e, g, _pick(t, (128,)))
    n_exp = p["peer_v"].shape[0]
    f = _peer_dense(h2f, p["peer_u_t"], p["peer_v"], w, _pick(t, (512, 256, 128)),
                    _pick(n_exp, (2048,)), 512)
    y = _final_ln(x1, f.reshape(b, s, d), mod, p["ln2_g"], p["ln2_b"], alpha, ts)
    return y, k, v, conv_state


def kernel(x_prompt, x_sample, cache_k, cache_v, state_conv, page_table, c_prompt, c_sample, rel_bias,
           w_ada, b_ada, w_in, lambda_q1, lambda_k1, lambda_q2, lambda_k2, subln_w, conv_w, w_out,
           ln1_g, ln1_b, w_pq, sub_keys, peer_u, peer_v, ln2_g, ln2_b):
    depth = w_ada.shape[0]
    bp, sp, d = x_prompt.shape
    bs, tsmp, _ = x_sample.shape
    alpha = (2.0 * depth) ** 0.25
    y_p, y_s = x_prompt, x_sample
    outs = [[] for _ in range(6)]
    zero_prefix = jnp.zeros((bp, CONV_K - 1, conv_w.shape[2]), x_prompt.dtype)
    c_all = jnp.concatenate([c_prompt, c_sample], axis=0)
    row = lambda a: a.reshape(1, -1)
    for layer in range(depth):
        lam_init = 0.8 - 0.6 * math.exp(-0.3 * layer)
        p = dict(w_in=w_in[layer].astype(BF16), conv_w=conv_w[layer], w_out=w_out[layer].astype(BF16),
                 ln1_g=row(ln1_g[layer]), ln1_b=row(ln1_b[layer]),
                 w_pq_t=w_pq[layer].T.astype(BF16),
                 sub_keys=sub_keys[layer].reshape(2 * PEER_HEADS, PEER_N_KEYS, -1),
                 peer_u_t=peer_u[layer].T.astype(BF16), peer_v=peer_v[layer].astype(BF16),
                 ln2_g=row(ln2_g[layer]), ln2_b=row(ln2_b[layer]))
        lam_args = (row(lambda_q1[layer]), row(lambda_k1[layer]), row(lambda_q2[layer]),
                    row(lambda_k2[layer]), row(subln_w[layer]))
        mod = _adaln(c_all, w_ada[layer], b_ada[layer])[:, None, :]
        attend_p = lambda q, k, v: _prompt_attention(q, k, v, rel_bias, *lam_args, lam_init, 256)
        attend_s = lambda q, k, v: _sample_attention(q, k, v, cache_k[layer], cache_v[layer], page_table,
                                                     rel_bias, *lam_args, lam_init,
                                                     _pick(page_table.shape[1], (8, 4, 2)))
        y_p, kp, vp, cp = _decoder_layer(y_p, mod[:bp], zero_prefix, attend_p, p, alpha)
        y_s, ks, vs, cs = _decoder_layer(y_s, mod[bp:], state_conv[layer], attend_s, p, alpha)
        for lst, val in zip(outs, (kp.reshape(bp, sp, N_HEADS, 2, HEAD_DIM), vp.reshape(bp, sp, N_HEADS, V_DIM), cp,
                                   ks.reshape(bs, tsmp, N_HEADS, 2, HEAD_DIM), vs.reshape(bs, tsmp, N_HEADS, V_DIM), cs)):
            lst.append(val)
    return (y_p, y_s) + tuple(jnp.stack(o) for o in outs)
```

```python
import functools
import math

import numpy as np
import jax
import jax.numpy as jnp
from jax import lax
from jax.experimental import pallas as pl
from jax.experimental.pallas import tpu as pltpu

N_HEADS = 4
HEAD_DIM = 64
V_DIM = 2 * HEAD_DIM
ATT_WIDTH = N_HEADS * V_DIM
QK_WIDTH = N_HEADS * 2 * HEAD_DIM
CONV_K = 3
NUM_BUCKETS = 32
MAX_DISTANCE = 128
PEER_HEADS = 8
PEER_N_KEYS = 128
PEER_TOPK = 16
LN_EPS = 1e-5

LANES = 128
SUBLANES = 8
VMEM_LIMIT_BYTES = 56 * 1024 * 1024

NEG = -1e30
BF16 = jnp.bfloat16
F32 = jnp.float32


def _cparams(sem):
    return pltpu.CompilerParams(dimension_semantics=sem, vmem_limit_bytes=VMEM_LIMIT_BYTES)


def _dot(a, b):
    return jnp.dot(a, b, preferred_element_type=F32)


def _dot_nt(a, b):
    return lax.dot_general(a, b, (((1,), (1,)), ((), ())), preferred_element_type=F32)


def _adaln_kernel(c_ref, w_ref, b_ref, o_ref):
    c = c_ref[...]
    s = c * jax.nn.sigmoid(c)
    o_ref[...] = _dot(s.astype(BF16), w_ref[...].astype(BF16)) + b_ref[...]


def _adaln(c, w_ada, b_ada):
    n, d = c.shape
    width = w_ada.shape[1]
    tn = width // 4
    return pl.pallas_call(
        _adaln_kernel,
        grid=(width // tn,),
        in_specs=[pl.BlockSpec((n, d), lambda j: (0, 0)),
                  pl.BlockSpec((d, tn), lambda j: (0, j)),
                  pl.BlockSpec((1, tn), lambda j: (0, j))],
        out_specs=pl.BlockSpec((n, tn), lambda j: (0, j)),
        out_shape=jax.ShapeDtypeStruct((n, width), F32),
        compiler_params=_cparams(("arbitrary",)),
        name="adaln",
    )(c, w_ada, b_ada.reshape(1, width))


def _inproj_kernel(x_ref, mod_ref, pre_ref, w_ref, cw_ref, q_ref, k_ref, v_ref, oc_ref, cs_ref, zbuf,
                   *, ts, d, cw):
    @pl.when(pl.program_id(1) == 0)
    def _():
        zbuf[SUBLANES - 2:SUBLANES, :] = pre_ref[0]

    shift1 = mod_ref[0, :, 0:d]
    scale1 = mod_ref[0, :, d:2 * d]
    h = x_ref[0] * (1.0 + scale1) + shift1
    proj = _dot(h.astype(BF16), w_ref[...])
    o1 = 2 * QK_WIDTH
    o2 = o1 + ATT_WIDTH
    q_ref[0] = proj[:, :QK_WIDTH]
    k_ref[0] = proj[:, QK_WIDTH:o1]
    v_ref[0] = proj[:, o1:o2]
    gb = proj[:, o2:o2 + cw]
    z = proj[:, o2 + cw:o2 + 2 * cw] * proj[:, o2 + 2 * cw:o2 + 3 * cw]
    zbuf[SUBLANES:SUBLANES + ts, :] = z
    y = (cw_ref[0:1, :] * zbuf[SUBLANES - 2:SUBLANES - 2 + ts, :]
         + cw_ref[1:2, :] * zbuf[SUBLANES - 1:SUBLANES - 1 + ts, :]
         + cw_ref[2:3, :] * z)
    oc_ref[0] = gb * y
    tail = zbuf[SUBLANES + ts - 2:SUBLANES + ts, :]
    cs_ref[0] = tail
    zbuf[SUBLANES - 2:SUBLANES, :] = tail


def _inproj(x, mod, prefix, w_in_bf, conv_w, ts):
    b, s, d = x.shape
    cw = conv_w.shape[1]
    pw = w_in_bf.shape[1]
    blk = lambda width: pl.BlockSpec((1, ts, width), lambda i, j: (i, j, 0))
    outs = pl.pallas_call(
        functools.partial(_inproj_kernel, ts=ts, d=d, cw=cw),
        grid=(b, s // ts),
        in_specs=[blk(d),
                  pl.BlockSpec((1, 1, mod.shape[2]), lambda i, j: (i, 0, 0)),
                  pl.BlockSpec((1, CONV_K - 1, cw), lambda i, j: (i, 0, 0)),
                  pl.BlockSpec((d, pw), lambda i, j: (0, 0)),
                  pl.BlockSpec((CONV_K, cw), lambda i, j: (0, 0))],
        out_specs=[blk(QK_WIDTH), blk(QK_WIDTH), blk(ATT_WIDTH), blk(cw),
                   pl.BlockSpec((1, CONV_K - 1, cw), lambda i, j: (i, 0, 0))],
        out_shape=[jax.ShapeDtypeStruct((b, s, QK_WIDTH), F32),
                   jax.ShapeDtypeStruct((b, s, QK_WIDTH), F32),
                   jax.ShapeDtypeStruct((b, s, ATT_WIDTH), F32),
                   jax.ShapeDtypeStruct((b, s, cw), F32),
                   jax.ShapeDtypeStruct((b, CONV_K - 1, cw), F32)],
        scratch_shapes=[pltpu.VMEM((SUBLANES + ts, cw), F32)],
        compiler_params=_cparams(("arbitrary", "arbitrary")),
        name="inproj",
    )(x, mod, prefix, w_in_bf, conv_w)
    return outs


def _t5_bucket_np(dist):
    dist = np.asarray(dist, np.int64)
    n = np.maximum(dist, 0)
    max_exact = NUM_BUCKETS // 2
    n_large = np.maximum(n, max_exact).astype(np.float64)
    large = max_exact + (np.log(n_large / max_exact) / math.log(MAX_DISTANCE / max_exact)
                         * (NUM_BUCKETS - max_exact)).astype(np.int64)
    large = np.minimum(large, NUM_BUCKETS - 1)
    bucket = np.where(n < max_exact, n, large)
    return np.where(dist < 0, -1, bucket).astype(np.int32)


def _bias_kernel(rb_ref, bk_ref, o_ref, *, rel_to):
    h = pl.program_id(0)
    bk = bk_ref[0]
    base = 0.0 if rel_to is None else rb_ref[rel_to, h]
    acc = jnp.where(bk < 0, NEG, 0.0).astype(F32)
    for b in range(NUM_BUCKETS):
        acc = jnp.where(bk == b, rb_ref[b, h] - base, acc)
    o_ref[0, 0] = acc


def _bias_tiles(rel_bias, buckets, rel_to=None):
    n, r, c = buckets.shape
    return pl.pallas_call(
        functools.partial(_bias_kernel, rel_to=rel_to),
        grid=(N_HEADS, n),
        in_specs=[pl.BlockSpec(memory_space=pltpu.SMEM),
                  pl.BlockSpec((1, r, c), lambda h, i: (i, 0, 0))],
        out_specs=pl.BlockSpec((1, 1, r, c), lambda h, i: (h, i, 0, 0)),
        out_shape=jax.ShapeDtypeStruct((N_HEADS, n, r, c), F32),
        compiler_params=_cparams(("arbitrary", "arbitrary")),
        name="bias_tiles",
    )(rel_bias, jnp.asarray(buckets))


def _far_bucket_from(dist_lo):
    b = _t5_bucket_np(np.arange(dist_lo, dist_lo + 4 * MAX_DISTANCE))
    assert (b == NUM_BUCKETS - 1).all()
    return NUM_BUCKETS - 1


def _diff_lambda(lq1_ref, lk1_ref, lq2_ref, lk2_ref, lam_init):
    a = jnp.sum(lq1_ref[...] * lk1_ref[...], axis=1, keepdims=True)
    b = jnp.sum(lq2_ref[...] * lk2_ref[...], axis=1, keepdims=True)
    return jnp.exp(a) - jnp.exp(b) + lam_init


def _split_q(q):
    lane = lax.broadcasted_iota(jnp.int32, q.shape, 1)
    q1 = jnp.where(lane < HEAD_DIM, q, 0.0)
    q2 = jnp.where(lane >= HEAD_DIM, q, 0.0)
    return jnp.concatenate([q1, q2], axis=0).astype(BF16)


def _online_update(s, v_bf, m_ref, l_ref, acc_ref, rows):
    m_old = m_ref[rows]
    m_new = jnp.maximum(m_old, jnp.max(s, axis=1, keepdims=True))
    p = jnp.exp(s - m_new)
    alpha = jnp.exp(m_old - m_new)
    l_ref[rows] = alpha * l_ref[rows] + jnp.sum(p, axis=1, keepdims=True)
    acc_ref[rows] = alpha * acc_ref[rows] + _dot(p.astype(BF16), v_bf)
    m_ref[rows] = m_new


def _diff_finish(acc, l, lam, sw, nq, lam_init):
    o = acc[:nq] / l[:nq] - lam * (acc[nq:] / l[nq:])
    o = o * lax.rsqrt(jnp.mean(o * o, axis=1, keepdims=True) + LN_EPS)
    return o * sw * (1.0 - lam_init)


def _pattn_kernel(q_ref, k_ref, v_ref, bt_ref, lq1, lk1, lq2, lk2, sw_ref, o_ref,
                  kb_ref, vt_ref, m_ref, l_ref, acc_ref, *, qb, lam_init):
    i = pl.program_id(2)
    n_kt = kb_ref.shape[0]

    @pl.when(i == 0)
    def _():
        for c in range(n_kt):
            kb_ref[c] = k_ref[0, c * qb:(c + 1) * qb, :].astype(BF16)
            vt_ref[c] = v_ref[0, c * qb:(c + 1) * qb, :].T.astype(BF16)

    q2 = _split_q(q_ref[0] * (HEAD_DIM ** -0.5))
    m_ref[...] = jnp.full(m_ref.shape, NEG, F32)
    l_ref[...] = jnp.zeros(l_ref.shape, F32)
    acc_ref[...] = jnp.zeros(acc_ref.shape, F32)

    def tiles(items):
        sts = []
        for j, bias in items:
            st = _dot_nt(kb_ref[j], q2)
            sts.append(st if bias is None else st + bias)
        m_old = m_ref[...]
        m_new = m_old
        for st in sts:
            m_new = jnp.maximum(m_new, jnp.max(st, axis=0, keepdims=True))
        alpha = jnp.exp(m_old - m_new)
        l_new = alpha * l_ref[...]
        acc_new = alpha * acc_ref[...]
        for (j, _), st in zip(items, sts):
            p = jnp.exp(st - m_new)
            l_new = l_new + jnp.sum(p, axis=0, keepdims=True)
            acc_new = acc_new + _dot(vt_ref[j], p.astype(BF16))
        l_ref[...] = l_new
        acc_ref[...] = acc_new
        m_ref[...] = m_new

    n_far = jnp.maximum(i - 1, 0)

    def far_pair(jj, carry):
        tiles([(2 * jj, None), (2 * jj + 1, None)])
        return carry

    lax.fori_loop(0, n_far // 2, far_pair, 0)

    @pl.when(n_far % 2 == 1)
    def _():
        tiles([(n_far - 1, None)])

    @pl.when(i >= 1)
    def _():
        tiles([(i - 1, bt_ref[0, 1]), (i, bt_ref[0, 0])])

    @pl.when(i == 0)
    def _():
        tiles([(i, bt_ref[0, 0])])
    lam = _diff_lambda(lq1, lk1, lq2, lk2, lam_init)
    acc = acc_ref[...]
    l = l_ref[...]
    ot = acc[:, :qb] / l[:, :qb] - lam * (acc[:, qb:] / l[:, qb:])
    ot = ot * lax.rsqrt(jnp.mean(ot * ot, axis=0, keepdims=True) + LN_EPS)
    o_ref[0] = ot.T * sw_ref[...] * (1.0 - lam_init)


def _prompt_attention(q, k, v, rel_bias, lq1, lk1, lq2, lk2, subln_w, lam_init, qb):
    b, s, _ = q.shape
    kk = np.arange(qb)[:, None]
    qq = np.tile(np.arange(qb), 2)[None, :]
    buckets = np.stack([_t5_bucket_np(qq - kk), _t5_bucket_np(qb + qq - kk)])
    far_bucket = _far_bucket_from(qb + 1)
    bt = _bias_tiles(rel_bias, buckets, rel_to=far_bucket)
    vec = lambda n: pl.BlockSpec((1, n), lambda bi, h, i: (0, 0))
    return pl.pallas_call(
        functools.partial(_pattn_kernel, qb=qb, lam_init=lam_init),
        grid=(b, N_HEADS, s // qb),
        in_specs=[pl.BlockSpec((1, qb, V_DIM), lambda bi, h, i: (bi, i, h)),
                  pl.BlockSpec((1, s, V_DIM), lambda bi, h, i: (bi, 0, h)),
                  pl.BlockSpec((1, s, V_DIM), lambda bi, h, i: (bi, 0, h)),
                  pl.BlockSpec((1, 2, qb, 2 * qb), lambda bi, h, i: (h, 0, 0, 0)),
                  vec(HEAD_DIM), vec(HEAD_DIM), vec(HEAD_DIM), vec(HEAD_DIM), vec(V_DIM)],
        out_specs=pl.BlockSpec((1, qb, V_DIM), lambda bi, h, i: (bi, i, h)),
        out_shape=jax.ShapeDtypeStruct((b, s, ATT_WIDTH), F32),
        scratch_shapes=[pltpu.VMEM((s // qb, qb, V_DIM), BF16), pltpu.VMEM((s // qb, V_DIM, qb), BF16),
                        pltpu.VMEM((1, 2 * qb), F32), pltpu.VMEM((1, 2 * qb), F32),
                        pltpu.VMEM((V_DIM, 2 * qb), F32)],
        compiler_params=_cparams(("arbitrary", "arbitrary", "arbitrary")),
        name="prompt_attention",
    )(q, k, v, bt, lq1, lk1, lq2, lk2, subln_w)


def _sattn_kernel(pt_ref, q_ref, kn_ref, vn_ref, bt_ref, bn_ref, lq1, lk1, lq2, lk2, sw_ref, *rest,
                  pps, t, lam_init):
    kp = rest[:pps]
    vp = rest[pps:2 * pps]
    o_ref = rest[2 * pps]
    m_ref, l_ref, acc_ref = rest[2 * pps + 1:]
    j = pl.program_id(1)
    nj = pl.num_programs(1)
    nq = 2 * t

    @pl.when(j == 0)
    def _():
        m_ref[...] = jnp.full(m_ref.shape, NEG, F32)
        l_ref[...] = jnp.zeros(l_ref.shape, F32)
        acc_ref[...] = jnp.zeros(acc_ref.shape, F32)

    q = q_ref[0] * (HEAD_DIM ** -0.5)
    for h in range(N_HEADS):
        cols = slice(h * V_DIM, (h + 1) * V_DIM)
        rows = slice(h * nq, (h + 1) * nq)
        q2 = _split_q(q[:, cols])
        s = jnp.concatenate([_dot_nt(q2, kp[r][0, :, cols].astype(BF16)) for r in range(pps)], axis=1)
        s = s + bt_ref[h, 0]
        m_old = m_ref[rows]
        m_new = jnp.maximum(m_old, jnp.max(s, axis=1, keepdims=True))
        p = jnp.exp(s - m_new)
        alpha = jnp.exp(m_old - m_new)
        l_ref[rows] = alpha * l_ref[rows] + jnp.sum(p, axis=1, keepdims=True)
        p = p.astype(BF16)
        pv = _dot(p[:, 0:LANES], vp[0][0, :, cols].astype(BF16))
        for r in range(1, pps):
            pv = pv + _dot(p[:, r * LANES:(r + 1) * LANES], vp[r][0, :, cols].astype(BF16))
        acc_ref[rows] = alpha * acc_ref[rows] + pv
        m_ref[rows] = m_new

    @pl.when(j == nj - 1)
    def _():
        lam = _diff_lambda(lq1, lk1, lq2, lk2, lam_init)
        for h in range(N_HEADS):
            cols = slice(h * V_DIM, (h + 1) * V_DIM)
            rows = slice(h * nq, (h + 1) * nq)
            q2 = _split_q(q[:, cols])
            s = _dot_nt(q2, kn_ref[0, :, cols].astype(BF16)) + bn_ref[h, 0]
            _online_update(s, vn_ref[0, :, cols].astype(BF16), m_ref, l_ref, acc_ref, rows)
            o_ref[0, :, cols] = _diff_finish(acc_ref[rows], l_ref[rows], lam, sw_ref[...], t, lam_init)


def _sample_attention(q, k_new, v_new, cache_k, cache_v, page_table, rel_bias,
                      lq1, lk1, lq2, lk2, subln_w, lam_init, pps):
    bs, t, _ = q.shape
    ck, cv = cache_k, cache_v
    page = ck.shape[1]
    n_pages = page_table.shape[1]
    past = n_pages * page
    nj = n_pages // pps
    chunk = pps * page
    tq = np.tile(np.arange(t), 2)[:, None]
    kc = np.arange(chunk)[None, :]
    last = _t5_bucket_np(past + tq - (past - chunk + kc))
    assert (_t5_bucket_np(past - (past - chunk) + 1 + np.arange(4 * MAX_DISTANCE)) == NUM_BUCKETS - 1).all()
    far = np.full_like(last, NUM_BUCKETS - 1)
    kn = np.arange(page)[None, :]
    newb = np.where(kn < t, _t5_bucket_np(tq - kn), -1).astype(np.int32)
    bt = _bias_tiles(rel_bias, np.stack([far, last]))
    bn = _bias_tiles(rel_bias, newb[None])
    knp = jnp.pad(k_new, ((0, 0), (0, page - t), (0, 0)))
    vnp = jnp.pad(v_new, ((0, 0), (0, page - t), (0, 0)))
    vec = lambda n: pl.BlockSpec((1, n), lambda b, j, pt: (0, 0))

    def page_spec(r):
        return pl.BlockSpec((1, page, QK_WIDTH), lambda b, j, pt: (pt[b, j * pps + r], 0, 0))

    grid_spec = pltpu.PrefetchScalarGridSpec(
        num_scalar_prefetch=1,
        grid=(bs, nj),
        in_specs=[pl.BlockSpec((1, t, QK_WIDTH), lambda b, j, pt: (b, 0, 0)),
                  pl.BlockSpec((1, page, QK_WIDTH), lambda b, j, pt: (b, 0, 0)),
                  pl.BlockSpec((1, page, ATT_WIDTH), lambda b, j, pt: (b, 0, 0)),
                  pl.BlockSpec((N_HEADS, 1, 2 * t, chunk), lambda b, j, pt: (0, (j == nj - 1).astype(jnp.int32), 0, 0)),
                  pl.BlockSpec((N_HEADS, 1, 2 * t, page), lambda b, j, pt: (0, 0, 0, 0)),
                  vec(HEAD_DIM), vec(HEAD_DIM), vec(HEAD_DIM), vec(HEAD_DIM), vec(V_DIM)]
                 + [page_spec(r) for r in range(pps)] + [page_spec(r) for r in range(pps)],
        out_specs=pl.BlockSpec((1, t, ATT_WIDTH), lambda b, j, pt: (b, 0, 0)),
        scratch_shapes=[pltpu.VMEM((N_HEADS * 2 * t, 1), F32), pltpu.VMEM((N_HEADS * 2 * t, 1), F32),
                        pltpu.VMEM((N_HEADS * 2 * t, V_DIM), F32)],
    )
    return pl.pallas_call(
        functools.partial(_sattn_kernel, pps=pps, t=t, lam_init=lam_init),
        grid_spec=grid_spec,
        out_shape=jax.ShapeDtypeStruct((bs, t, ATT_WIDTH), F32),
        compiler_params=_cparams(("arbitrary", "arbitrary")),
        name="sample_attention",
    )(page_table, q, knp, vnp, bt, bn, lq1, lk1, lq2, lk2, subln_w, *([ck] * pps), *([cv] * pps))


def _layer_norm(y, g, b):
    mu = jnp.mean(y, axis=1, keepdims=True)
    yc = y - mu
    var = jnp.mean(yc * yc, axis=1, keepdims=True)
    return yc * lax.rsqrt(var + LN_EPS) * g + b


def _outproj_kernel(oa_ref, oc_ref, x_ref, mod_ref, w_ref, g_ref, b_ref, x1_ref, h2_ref, *, d, alpha):
    aw = oa_ref.shape[2]
    mix = _dot(oa_ref[0].astype(BF16), w_ref[0:aw, :]) + _dot(oc_ref[0].astype(BF16), w_ref[aw:, :])
    gate1 = mod_ref[0, :, 2 * d:3 * d]
    shift2 = mod_ref[0, :, 3 * d:4 * d]
    scale2 = mod_ref[0, :, 4 * d:5 * d]
    x1 = _layer_norm(alpha * x_ref[0] + gate1 * mix, g_ref[...], b_ref[...])
    x1_ref[0] = x1
    h2_ref[0] = x1 * (1.0 + scale2) + shift2


def _outproj(oa, oc, x, mod, w_out_bf, g, bb, alpha, ts):
    b, s, d = x.shape
    blk = lambda width: pl.BlockSpec((1, ts, width), lambda i, j: (i, j, 0))
    vec = pl.BlockSpec((1, d), lambda i, j: (0, 0))
    return pl.pallas_call(
        functools.partial(_outproj_kernel, d=d, alpha=alpha),
        grid=(b, s // ts),
        in_specs=[blk(oa.shape[2]), blk(oc.shape[2]), blk(d),
                  pl.BlockSpec((1, 1, mod.shape[2]), lambda i, j: (i, 0, 0)),
                  pl.BlockSpec(w_out_bf.shape, lambda i, j: (0, 0)), vec, vec],
        out_specs=[blk(d), blk(d)],
        out_shape=[jax.ShapeDtypeStruct((b, s, d), F32), jax.ShapeDtypeStruct((b, s, d), F32)],
        compiler_params=_cparams(("arbitrary", "arbitrary")),
        name="outproj_ln1",
    )(oa, oc, x, mod, w_out_bf, g, bb)


_CAND_BLOCKS = [(i, 16 if i == 0 else 8, PEER_TOPK // (i + 1)) for i in range(8)]


def _top16_rows(x, iota_n, n, v_ref, i_ref):
    for r in range(PEER_TOPK):
        m = jnp.max(x, axis=0, keepdims=True)
        idx = jnp.min(jnp.where(x == m, iota_n, n), axis=0, keepdims=True)
        v_ref[r:r + 1, :] = m
        i_ref[r:r + 1, :] = idx
        x = jnp.where(iota_n == idx, -jnp.inf, x)


def _route_kernel(h_ref, w_ref, sk_ref, e_ref, g_ref, qt_ref, sv_ref, si_ref, cv_ref, ce_ref, et_ref, gt_ref,
                  *, tb):
    nlb = tb // LANES
    hb = h_ref[...].astype(BF16)
    qt_ref[...] = _dot_nt(w_ref[...], hb)
    iota_n = lax.broadcasted_iota(jnp.int32, (PEER_N_KEYS, LANES), 0)
    half = sk_ref.shape[2]

    def stage1(hp, carry):
        row0 = pl.multiple_of(hp * half, half)
        st = _dot(sk_ref[hp].astype(BF16), qt_ref[pl.ds(row0, half), :].astype(BF16))
        for lb in range(nlb):
            _top16_rows(st[:, lb * LANES:(lb + 1) * LANES], iota_n, PEER_N_KEYS,
                        sv_ref.at[hp, :, lb * LANES:(lb + 1) * LANES],
                        si_ref.at[hp, :, lb * LANES:(lb + 1) * LANES])
        return carry

    lax.fori_loop(0, 2 * PEER_HEADS, stage1, 0)

    sub8 = lax.broadcasted_iota(jnp.int32, (8, LANES), 0)
    sub16 = lax.broadcasted_iota(jnp.int32, (16, LANES), 0)

    def stage2(h, carry):
        for lb in range(nlb):
            ls = slice(lb * LANES, (lb + 1) * LANES)
            sv0 = sv_ref[2 * h, :, ls]
            sv1 = sv_ref[2 * h + 1, :, ls]
            si0 = si_ref[2 * h, :, ls]
            si1 = si_ref[2 * h + 1, :, ls]
            vals, flats, es = [], [], []
            for i, rows, cnt in _CAND_BLOCKS:
                sub = sub16 if rows == 16 else sub8
                vals.append(jnp.where(sub < cnt, sv0[i:i + 1] + sv1[0:rows], -jnp.inf))
                flats.append(i * PEER_TOPK + sub)
                es.append(si0[i:i + 1] * PEER_N_KEYS + si1[0:rows])
            vals.append(sv0[8:16] + sv1[0:1])
            flats.append((sub8 + 8) * PEER_TOPK)
            es.append(si0[8:16] * PEER_N_KEYS + si1[0:1])
            cand = jnp.concatenate(vals, axis=0)
            flat = jnp.concatenate(flats, axis=0)
            ecand = jnp.concatenate(es, axis=0)
            big = PEER_TOPK * PEER_TOPK
            for r in range(PEER_TOPK):
                m = jnp.max(cand, axis=0, keepdims=True)
                fl = jnp.min(jnp.where(cand == m, flat, big), axis=0, keepdims=True)
                hit = flat == fl
                cv_ref[r:r + 1, :] = m
                ce_ref[r:r + 1, :] = jnp.max(jnp.where(hit, ecand, -1), axis=0, keepdims=True)
                cand = jnp.where(hit, -jnp.inf, cand)
            cv = cv_ref[...]
            ex = jnp.exp(cv - cv[0:1])
            g = ex / jnp.sum(ex, axis=0, keepdims=True)
            row0 = pl.multiple_of(h * PEER_TOPK, PEER_TOPK)
            gt_ref[pl.ds(row0, PEER_TOPK), ls] = g
            et_ref[pl.ds(row0, PEER_TOPK), ls] = ce_ref[...]
        return carry

    lax.fori_loop(0, PEER_HEADS, stage2, 0)
    e_ref[...] = et_ref[...].T
    g_ref[...] = gt_ref[...].T


def _peer_route(h2, w_pq_t_bf, sub_keys, tb):
    t, d = h2.shape
    qw = w_pq_t_bf.shape[0]
    hp, n_keys, half = sub_keys.shape
    slots = PEER_HEADS * PEER_TOPK
    return pl.pallas_call(
        functools.partial(_route_kernel, tb=tb),
        grid=(t // tb,),
        in_specs=[pl.BlockSpec((tb, d), lambda i: (i, 0)),
                  pl.BlockSpec((qw, d), lambda i: (0, 0)),
                  pl.BlockSpec((hp, n_keys, half), lambda i: (0, 0, 0))],
        out_specs=[pl.BlockSpec((tb, slots), lambda i: (i, 0)), pl.BlockSpec((tb, slots), lambda i: (i, 0))],
        out_shape=[jax.ShapeDtypeStruct((t, slots), jnp.int32), jax.ShapeDtypeStruct((t, slots), F32)],
        scratch_shapes=[pltpu.VMEM((qw, tb), F32),
                        pltpu.VMEM((hp, PEER_TOPK, tb), F32), pltpu.VMEM((hp, PEER_TOPK, tb), jnp.int32),
                        pltpu.VMEM((PEER_TOPK, LANES), F32), pltpu.VMEM((PEER_TOPK, LANES), jnp.int32),
                        pltpu.VMEM((slots, tb), jnp.int32), pltpu.VMEM((slots, tb), F32)],
        compiler_params=_cparams(("arbitrary",)),
        name="peer_route",
    )(h2, w_pq_t_bf, sub_keys)


def _expand_kernel(e_ref, g_ref, w_ref, *, tb):
    iota = lax.broadcasted_iota(jnp.int32, (PEER_N_KEYS, e_ref.shape[1]), 0)

    def body(t, carry):
        er = e_ref[pl.ds(t, 1), :]
        gr = g_ref[pl.ds(t, 1), :]
        ea = er // PEER_N_KEYS
        eb = er - ea * PEER_N_KEYS
        oa = jnp.where(iota == ea, gr, 0.0).astype(BF16)
        ob = jnp.where(iota == eb, 1.0, 0.0).astype(BF16)
        w_ref[t] = _dot_nt(oa, ob)
        return carry

    lax.fori_loop(0, tb, body, 0, unroll=8)


def _peer_expand(e, g, tb):
    t, slots = e.shape
    return pl.pallas_call(
        functools.partial(_expand_kernel, tb=tb),
        grid=(t // tb,),
        in_specs=[pl.BlockSpec((tb, slots), lambda i: (i, 0)), pl.BlockSpec((tb, slots), lambda i: (i, 0))],
        out_specs=pl.BlockSpec((tb, PEER_N_KEYS, PEER_N_KEYS), lambda i: (i, 0, 0)),
        out_shape=jax.ShapeDtypeStruct((t, PEER_N_KEYS, PEER_N_KEYS), F32),
        compiler_params=_cparams(("arbitrary",)),
        name="peer_expand",
    )(e, g)


def _dense_kernel(h_ref, ut_ref, v_ref, w_ref, f_ref, hb_ref, *, eb, ec):
    j = pl.program_id(1)

    @pl.when(j == 0)
    def _():
        hb_ref[...] = h_ref[...].astype(BF16)
        f_ref[...] = jnp.zeros(f_ref.shape, F32)

    hb = hb_ref[...]
    for c in range(eb // ec):
        a = _dot(hb, ut_ref[:, c * ec:(c + 1) * ec])
        a = 0.5 * a * (1.0 + lax.erf(a * (2.0 ** -0.5)))
        parts = []
        for al in range(ec // PEER_N_KEYS):
            ai = c * (ec // PEER_N_KEYS) + al
            parts.append((a[:, al * PEER_N_KEYS:(al + 1) * PEER_N_KEYS] * w_ref[:, ai, :]).astype(BF16))
        wa = jnp.concatenate(parts, axis=1)
        f_ref[...] += _dot(wa, v_ref[c * ec:(c + 1) * ec, :])


def _peer_dense(h2, u_t_bf, v_bf, w, tb, eb, ec):
    t, d = h2.shape
    n_exp = v_bf.shape[0]
    return pl.pallas_call(
        functools.partial(_dense_kernel, eb=eb, ec=ec),
        grid=(t // tb, n_exp // eb),
        in_specs=[pl.BlockSpec((tb, d), lambda i, j: (i, 0)),
                  pl.BlockSpec((d, eb), lambda i, j: (0, j)),
                  pl.BlockSpec((eb, d), lambda i, j: (j, 0)),
                  pl.BlockSpec((tb, eb // PEER_N_KEYS, PEER_N_KEYS), lambda i, j: (i, j, 0))],
        out_specs=pl.BlockSpec((tb, d), lambda i, j: (i, 0)),
        out_shape=jax.ShapeDtypeStruct((t, d), F32),
        scratch_shapes=[pltpu.VMEM((tb, d), BF16)],
        compiler_params=_cparams(("arbitrary", "arbitrary")),
        name="peer_dense",
    )(h2, u_t_bf, v_bf, w)


def _final_kernel(x1_ref, f_ref, mod_ref, g_ref, b_ref, y_ref, *, d, alpha):
    gate2 = mod_ref[0, :, 5 * d:6 * d]
    y_ref[0] = _layer_norm(alpha * x1_ref[0] + gate2 * f_ref[0], g_ref[...], b_ref[...])


def _final_ln(x1, f, mod, g, bb, alpha, ts):
    b, s, d = x1.shape
    blk = pl.BlockSpec((1, ts, d), lambda i, j: (i, j, 0))
    vec = pl.BlockSpec((1, d), lambda i, j: (0, 0))
    return pl.pallas_call(
        functools.partial(_final_kernel, d=d, alpha=alpha),
        grid=(b, s // ts),
        in_specs=[blk, blk, pl.BlockSpec((1, 1, mod.shape[2]), lambda i, j: (i, 0, 0)), vec, vec],
        out_specs=blk,
        out_shape=jax.ShapeDtypeStruct((b, s, d), F32),
        compiler_params=_cparams(("arbitrary", "arbitrary")),
        name="final_ln2",
    )(x1, f, mod, g, bb)


def _pick(n, prefs):
    for p in prefs:
        if n % p == 0:
            return p
    return n


def _decoder_layer(x, mod, prefix, attend, p, alpha):
    b, s, d = x.shape
    ts = _pick(s, (512, 256, 128))
    q, k, v, oc, conv_state = _inproj(x, mod, prefix, p["w_in"], p["conv_w"], ts)
    oa = attend(q, k, v)
    x1, h2 = _outproj(oa, oc, x, mod, p["w_out"], p["ln1_g"], p["ln1_b"], alpha, ts)
    t = b * s
    h2f = h2.reshape(t, d)
    e, g = _peer_route(h2f, p["w_pq_t"], p["sub_keys"], _pick(t, (256, 128)))
    w = _peer_expand(e, g, _pick(t, (128,)))
    n_exp = p["peer_v"].shape[0]
    f = _peer_dense(h2f, p["peer_u_t"], p["peer_v"], w, _pick(t, (512, 256, 128)),
                    _pick(n_exp, (2048,)), 1024)
    y = _final_ln(x1, f.reshape(b, s, d), mod, p["ln2_g"], p["ln2_b"], alpha, ts)
    return y, k, v, conv_state


def kernel(x_prompt, x_sample, cache_k, cache_v, state_conv, page_table, c_prompt, c_sample, rel_bias,
           w_ada, b_ada, w_in, lambda_q1, lambda_k1, lambda_q2, lambda_k2, subln_w, conv_w, w_out,
           ln1_g, ln1_b, w_pq, sub_keys, peer_u, peer_v, ln2_g, ln2_b):
    depth = w_ada.shape[0]
    bp, sp, d = x_prompt.shape
    bs, tsmp, _ = x_sample.shape
    alpha = (2.0 * depth) ** 0.25
    y_p, y_s = x_prompt, x_sample
    outs = [[] for _ in range(6)]
    zero_prefix = jnp.zeros((bp, CONV_K - 1, conv_w.shape[2]), x_prompt.dtype)
    c_all = jnp.concatenate([c_prompt, c_sample], axis=0)
    n_pool, page = cache_k.shape[1], cache_k.shape[2]
    ck_all = cache_k.reshape(depth * n_pool, page, QK_WIDTH)
    cv_all = cache_v.reshape(depth * n_pool, page, ATT_WIDTH)
    row = lambda a: a.reshape(1, -1)
    for layer in range(depth):
        lam_init = 0.8 - 0.6 * math.exp(-0.3 * layer)
        p = dict(w_in=w_in[layer].astype(BF16), conv_w=conv_w[layer], w_out=w_out[layer].astype(BF16),
                 ln1_g=row(ln1_g[layer]), ln1_b=row(ln1_b[layer]),
                 w_pq_t=w_pq[layer].T.astype(BF16),
                 sub_keys=sub_keys[layer].reshape(2 * PEER_HEADS, PEER_N_KEYS, -1),
                 peer_u_t=peer_u[layer].T.astype(BF16), peer_v=peer_v[layer].astype(BF16),
                 ln2_g=row(ln2_g[layer]), ln2_b=row(ln2_b[layer]))
        lam_args = (row(lambda_q1[layer]), row(lambda_k1[layer]), row(lambda_q2[layer]),
                    row(lambda_k2[layer]), row(subln_w[layer]))
        mod = _adaln(c_all, w_ada[layer], b_ada[layer])[:, None, :]
        attend_p = lambda q, k, v: _prompt_attention(q, k, v, rel_bias, *lam_args, lam_init, 256)
        pt_layer = page_table + layer * n_pool
        attend_s = lambda q, k, v: _sample_attention(q, k, v, ck_all, cv_all, pt_layer,
                                                     rel_bias, *lam_args, lam_init,
                                                     _pick(page_table.shape[1], (8, 4, 2)))
        y_p, kp, vp, cp = _decoder_layer(y_p, mod[:bp], zero_prefix, attend_p, p, alpha)
        y_s, ks, vs, cs = _decoder_layer(y_s, mod[bp:], state_conv[layer], attend_s, p, alpha)
        for lst, val in zip(outs, (kp.reshape(bp, sp, N_HEADS, 2, HEAD_DIM), vp.reshape(bp, sp, N_HEADS, V_DIM), cp,
                                   ks.reshape(bs, tsmp, N_HEADS, 2, HEAD_DIM), vs.reshape(bs, tsmp, N_HEADS, V_DIM), cs)):
            lst.append(val)
    return (y_p, y_s) + tuple(jnp.stack(o) for o in outs)
```

```python
import functools
import math

import numpy as np
import jax
import jax.numpy as jnp
from jax import lax
from jax.experimental import pallas as pl
from jax.experimental.pallas import tpu as pltpu

N_HEADS = 4
HEAD_DIM = 64
V_DIM = 2 * HEAD_DIM
ATT_WIDTH = N_HEADS * V_DIM
QK_WIDTH = N_HEADS * 2 * HEAD_DIM
CONV_K = 3
NUM_BUCKETS = 32
MAX_DISTANCE = 128
PEER_HEADS = 8
PEER_N_KEYS = 128
PEER_TOPK = 16
LN_EPS = 1e-5

LANES = 128
SUBLANES = 8
VMEM_LIMIT_BYTES = 56 * 1024 * 1024

NEG = -1e30
BF16 = jnp.bfloat16
F32 = jnp.float32


def _cparams(sem):
    return pltpu.CompilerParams(dimension_semantics=sem, vmem_limit_bytes=VMEM_LIMIT_BYTES)


def _dot(a, b):
    return jnp.dot(a, b, preferred_element_type=F32)


def _dot_nt(a, b):
    return lax.dot_general(a, b, (((1,), (1,)), ((), ())), preferred_element_type=F32)


def _adaln_kernel(c_ref, w_ref, b_ref, o_ref):
    c = c_ref[...]
    s = c * jax.nn.sigmoid(c)
    o_ref[...] = _dot(s.astype(BF16), w_ref[...].astype(BF16)) + b_ref[...]


def _adaln(c, w_ada, b_ada):
    n, d = c.shape
    width = w_ada.shape[1]
    tn = width // 4
    return pl.pallas_call(
        _adaln_kernel,
        grid=(width // tn,),
        in_specs=[pl.BlockSpec((n, d), lambda j: (0, 0)),
                  pl.BlockSpec((d, tn), lambda j: (0, j)),
                  pl.BlockSpec((1, tn), lambda j: (0, j))],
        out_specs=pl.BlockSpec((n, tn), lambda j: (0, j)),
        out_shape=jax.ShapeDtypeStruct((n, width), F32),
        compiler_params=_cparams(("arbitrary",)),
        name="adaln",
    )(c, w_ada, b_ada.reshape(1, width))


def _inproj_kernel(x_ref, mod_ref, pre_ref, w_ref, cw_ref, q_ref, k_ref, v_ref, oc_ref, cs_ref, zbuf,
                   *, ts, d, cw):
    @pl.when(pl.program_id(1) == 0)
    def _():
        zbuf[SUBLANES - 2:SUBLANES, :] = pre_ref[0]

    shift1 = mod_ref[0, :, 0:d]
    scale1 = mod_ref[0, :, d:2 * d]
    h = x_ref[0] * (1.0 + scale1) + shift1
    proj = _dot(h.astype(BF16), w_ref[...])
    o1 = 2 * QK_WIDTH
    o2 = o1 + ATT_WIDTH
    q_ref[0] = proj[:, :QK_WIDTH]
    k_ref[0] = proj[:, QK_WIDTH:o1]
    v_ref[0] = proj[:, o1:o2]
    gb = proj[:, o2:o2 + cw]
    z = proj[:, o2 + cw:o2 + 2 * cw] * proj[:, o2 + 2 * cw:o2 + 3 * cw]
    zbuf[SUBLANES:SUBLANES + ts, :] = z
    y = (cw_ref[0:1, :] * zbuf[SUBLANES - 2:SUBLANES - 2 + ts, :]
         + cw_ref[1:2, :] * zbuf[SUBLANES - 1:SUBLANES - 1 + ts, :]
         + cw_ref[2:3, :] * z)
    oc_ref[0] = gb * y
    tail = zbuf[SUBLANES + ts - 2:SUBLANES + ts, :]
    cs_ref[0] = tail
    zbuf[SUBLANES - 2:SUBLANES, :] = tail


def _inproj(x, mod, prefix, w_in_bf, conv_w, ts):
    b, s, d = x.shape
    cw = conv_w.shape[1]
    pw = w_in_bf.shape[1]
    blk = lambda width: pl.BlockSpec((1, ts, width), lambda i, j: (i, j, 0))
    outs = pl.pallas_call(
        functools.partial(_inproj_kernel, ts=ts, d=d, cw=cw),
        grid=(b, s // ts),
        in_specs=[blk(d),
                  pl.BlockSpec((1, 1, mod.shape[2]), lambda i, j: (i, 0, 0)),
                  pl.BlockSpec((1, CONV_K - 1, cw), lambda i, j: (i, 0, 0)),
                  pl.BlockSpec((d, pw), lambda i, j: (0, 0)),
                  pl.BlockSpec((CONV_K, cw), lambda i, j: (0, 0))],
        out_specs=[blk(QK_WIDTH), blk(QK_WIDTH), blk(ATT_WIDTH), blk(cw),
                   pl.BlockSpec((1, CONV_K - 1, cw), lambda i, j: (i, 0, 0))],
        out_shape=[jax.ShapeDtypeStruct((b, s, QK_WIDTH), F32),
                   jax.ShapeDtypeStruct((b, s, QK_WIDTH), F32),
                   jax.ShapeDtypeStruct((b, s, ATT_WIDTH), F32),
                   jax.ShapeDtypeStruct((b, s, cw), F32),
                   jax.ShapeDtypeStruct((b, CONV_K - 1, cw), F32)],
        scratch_shapes=[pltpu.VMEM((SUBLANES + ts, cw), F32)],
        compiler_params=_cparams(("arbitrary", "arbitrary")),
        name="inproj",
    )(x, mod, prefix, w_in_bf, conv_w)
    return outs


def _t5_bucket_np(dist):
    dist = np.asarray(dist, np.int64)
    n = np.maximum(dist, 0)
    max_exact = NUM_BUCKETS // 2
    n_large = np.maximum(n, max_exact).astype(np.float64)
    large = max_exact + (np.log(n_large / max_exact) / math.log(MAX_DISTANCE / max_exact)
                         * (NUM_BUCKETS - max_exact)).astype(np.int64)
    large = np.minimum(large, NUM_BUCKETS - 1)
    bucket = np.where(n < max_exact, n, large)
    return np.where(dist < 0, -1, bucket).astype(np.int32)


def _bias_kernel(rb_ref, bk_ref, o_ref, *, rel_to):
    h = pl.program_id(0)
    bk = bk_ref[0]
    base = 0.0 if rel_to is None else rb_ref[rel_to, h]
    acc = jnp.where(bk < 0, NEG, 0.0).astype(F32)
    for b in range(NUM_BUCKETS):
        acc = jnp.where(bk == b, rb_ref[b, h] - base, acc)
    o_ref[0, 0] = acc


def _bias_tiles(rel_bias, buckets, rel_to=None):
    n, r, c = buckets.shape
    return pl.pallas_call(
        functools.partial(_bias_kernel, rel_to=rel_to),
        grid=(N_HEADS, n),
        in_specs=[pl.BlockSpec(memory_space=pltpu.SMEM),
                  pl.BlockSpec((1, r, c), lambda h, i: (i, 0, 0))],
        out_specs=pl.BlockSpec((1, 1, r, c), lambda h, i: (h, i, 0, 0)),
        out_shape=jax.ShapeDtypeStruct((N_HEADS, n, r, c), F32),
        compiler_params=_cparams(("arbitrary", "arbitrary")),
        name="bias_tiles",
    )(rel_bias, jnp.asarray(buckets))


def _far_bucket_from(dist_lo):
    b = _t5_bucket_np(np.arange(dist_lo, dist_lo + 4 * MAX_DISTANCE))
    assert (b == NUM_BUCKETS - 1).all()
    return NUM_BUCKETS - 1


def _diff_lambda(lq1_ref, lk1_ref, lq2_ref, lk2_ref, lam_init):
    a = jnp.sum(lq1_ref[...] * lk1_ref[...], axis=1, keepdims=True)
    b = jnp.sum(lq2_ref[...] * lk2_ref[...], axis=1, keepdims=True)
    return jnp.exp(a) - jnp.exp(b) + lam_init


def _split_q(q):
    lane = lax.broadcasted_iota(jnp.int32, q.shape, 1)
    q1 = jnp.where(lane < HEAD_DIM, q, 0.0)
    q2 = jnp.where(lane >= HEAD_DIM, q, 0.0)
    return jnp.concatenate([q1, q2], axis=0).astype(BF16)


def _online_update(s, v_bf, m_ref, l_ref, acc_ref, rows):
    m_old = m_ref[rows]
    m_new = jnp.maximum(m_old, jnp.max(s, axis=1, keepdims=True))
    p = jnp.exp(s - m_new)
    alpha = jnp.exp(m_old - m_new)
    l_ref[rows] = alpha * l_ref[rows] + jnp.sum(p, axis=1, keepdims=True)
    acc_ref[rows] = alpha * acc_ref[rows] + _dot(p.astype(BF16), v_bf)
    m_ref[rows] = m_new


def _diff_finish(acc, l, lam, sw, nq, lam_init):
    o = acc[:nq] / l[:nq] - lam * (acc[nq:] / l[nq:])
    o = o * lax.rsqrt(jnp.mean(o * o, axis=1, keepdims=True) + LN_EPS)
    return o * sw * (1.0 - lam_init)


def _pattn_kernel(q_ref, k_ref, v_ref, bt_ref, lq1, lk1, lq2, lk2, sw_ref, o_ref,
                  kb_ref, vt_ref, m_ref, l_ref, acc_ref, *, qb, nh, lam_init):
    i = pl.program_id(2)
    n_kt = kb_ref.shape[1]

    @pl.when(i == 0)
    def _():
        for g in range(nh):
            cols = slice(g * V_DIM, (g + 1) * V_DIM)
            for c in range(n_kt):
                kb_ref[g, c] = k_ref[0, c * qb:(c + 1) * qb, cols].astype(BF16)
                vt_ref[g, c] = v_ref[0, c * qb:(c + 1) * qb, cols].T.astype(BF16)

    q2 = [_split_q(q_ref[0, :, g * V_DIM:(g + 1) * V_DIM] * (HEAD_DIM ** -0.5)) for g in range(nh)]
    m_ref[...] = jnp.full(m_ref.shape, NEG, F32)
    l_ref[...] = jnp.zeros(l_ref.shape, F32)
    acc_ref[...] = jnp.zeros(acc_ref.shape, F32)

    def tiles(items):
        old = [(m_ref[g], l_ref[g], acc_ref[g]) for g in range(nh)]
        new = []
        for g in range(nh):
            sts = []
            for j, slot in items:
                st = _dot_nt(kb_ref[g, j], q2[g])
                sts.append(st if slot is None else st + bt_ref[g, slot])
            m_old, l_old, acc_old = old[g]
            m_new = m_old
            for st in sts:
                m_new = jnp.maximum(m_new, jnp.max(st, axis=0, keepdims=True))
            alpha = jnp.exp(m_old - m_new)
            l_new = alpha * l_old
            acc_new = alpha * acc_old
            for (j, _), st in zip(items, sts):
                p = jnp.exp(st - m_new)
                l_new = l_new + jnp.sum(p, axis=0, keepdims=True)
                acc_new = acc_new + _dot(vt_ref[g, j], p.astype(BF16))
            new.append((m_new, l_new, acc_new))
        for g in range(nh):
            m_ref[g], l_ref[g], acc_ref[g] = new[g]

    n_far = jnp.maximum(i - 1, 0)

    def far_pair(jj, carry):
        tiles([(2 * jj, None), (2 * jj + 1, None)])
        return carry

    lax.fori_loop(0, n_far // 2, far_pair, 0)

    @pl.when(n_far % 2 == 1)
    def _():
        tiles([(n_far - 1, None)])

    @pl.when(i >= 1)
    def _():
        tiles([(i - 1, 1), (i, 0)])

    @pl.when(i == 0)
    def _():
        tiles([(i, 0)])

    lam = _diff_lambda(lq1, lk1, lq2, lk2, lam_init)
    for g in range(nh):
        acc = acc_ref[g]
        l = l_ref[g]
        ot = acc[:, :qb] / l[:, :qb] - lam * (acc[:, qb:] / l[:, qb:])
        ot = ot * lax.rsqrt(jnp.mean(ot * ot, axis=0, keepdims=True) + LN_EPS)
        o_ref[0, :, g * V_DIM:(g + 1) * V_DIM] = ot.T * sw_ref[...] * (1.0 - lam_init)


def _prompt_attention(q, k, v, rel_bias, lq1, lk1, lq2, lk2, subln_w, lam_init, qb, nh):
    b, s, _ = q.shape
    kk = np.arange(qb)[:, None]
    qq = np.tile(np.arange(qb), 2)[None, :]
    buckets = np.stack([_t5_bucket_np(qq - kk), _t5_bucket_np(qb + qq - kk)])
    far_bucket = _far_bucket_from(qb + 1)
    bt = _bias_tiles(rel_bias, buckets, rel_to=far_bucket)
    vec = lambda n: pl.BlockSpec((1, n), lambda bi, h, i: (0, 0))
    gw = nh * V_DIM
    n_kt = s // qb
    return pl.pallas_call(
        functools.partial(_pattn_kernel, qb=qb, nh=nh, lam_init=lam_init),
        grid=(b, N_HEADS // nh, n_kt),
        in_specs=[pl.BlockSpec((1, qb, gw), lambda bi, h, i: (bi, i, h)),
                  pl.BlockSpec((1, s, gw), lambda bi, h, i: (bi, 0, h)),
                  pl.BlockSpec((1, s, gw), lambda bi, h, i: (bi, 0, h)),
                  pl.BlockSpec((nh, 2, qb, 2 * qb), lambda bi, h, i: (h, 0, 0, 0)),
                  vec(HEAD_DIM), vec(HEAD_DIM), vec(HEAD_DIM), vec(HEAD_DIM), vec(V_DIM)],
        out_specs=pl.BlockSpec((1, qb, gw), lambda bi, h, i: (bi, i, h)),
        out_shape=jax.ShapeDtypeStruct((b, s, ATT_WIDTH), F32),
        scratch_shapes=[pltpu.VMEM((nh, n_kt, qb, V_DIM), BF16), pltpu.VMEM((nh, n_kt, V_DIM, qb), BF16),
                        pltpu.VMEM((nh, 1, 2 * qb), F32), pltpu.VMEM((nh, 1, 2 * qb), F32),
                        pltpu.VMEM((nh, V_DIM, 2 * qb), F32)],
        compiler_params=_cparams(("arbitrary", "arbitrary", "arbitrary")),
        name="prompt_attention",
    )(q, k, v, bt, lq1, lk1, lq2, lk2, subln_w)


def _sattn_kernel(pt_ref, q_ref, kn_ref, vn_ref, bt_ref, bn_ref, lq1, lk1, lq2, lk2, sw_ref, *rest,
                  pps, t, lam_init):
    kp = rest[:pps]
    vp = rest[pps:2 * pps]
    o_ref = rest[2 * pps]
    m_ref, l_ref, acc_ref = rest[2 * pps + 1:]
    j = pl.program_id(1)
    nj = pl.num_programs(1)
    nq = 2 * t

    @pl.when(j == 0)
    def _():
        m_ref[...] = jnp.full(m_ref.shape, NEG, F32)
        l_ref[...] = jnp.zeros(l_ref.shape, F32)
        acc_ref[...] = jnp.zeros(acc_ref.shape, F32)

    q = q_ref[0] * (HEAD_DIM ** -0.5)
    m_all, l_all, acc_all = m_ref[...], l_ref[...], acc_ref[...]
    new_m, new_l, new_acc = [], [], []
    for h in range(N_HEADS):
        cols = slice(h * V_DIM, (h + 1) * V_DIM)
        rows = slice(h * nq, (h + 1) * nq)
        q2 = _split_q(q[:, cols])
        s = jnp.concatenate([_dot(q2, kp[r][0, h].reshape(V_DIM, -1).astype(BF16)) for r in range(pps)],
                            axis=1)
        s = s + bt_ref[h, 0]
        m_old = m_all[rows]
        m_new = jnp.maximum(m_old, jnp.max(s, axis=1, keepdims=True))
        p = jnp.exp(s - m_new)
        alpha = jnp.exp(m_old - m_new)
        new_l.append(alpha * l_all[rows] + jnp.sum(p, axis=1, keepdims=True))
        p = p.astype(BF16)
        vrow = pl.ds(h, LANES, stride=N_HEADS)
        pv = _dot(p[:, 0:LANES], vp[0][vrow, :].astype(BF16))
        for r in range(1, pps):
            pv = pv + _dot(p[:, r * LANES:(r + 1) * LANES], vp[r][vrow, :].astype(BF16))
        new_acc.append(alpha * acc_all[rows] + pv)
        new_m.append(m_new)
    m_ref[...] = jnp.concatenate(new_m, axis=0)
    l_ref[...] = jnp.concatenate(new_l, axis=0)
    acc_ref[...] = jnp.concatenate(new_acc, axis=0)

    @pl.when(j == nj - 1)
    def _():
        lam = _diff_lambda(lq1, lk1, lq2, lk2, lam_init)
        for h in range(N_HEADS):
            cols = slice(h * V_DIM, (h + 1) * V_DIM)
            rows = slice(h * nq, (h + 1) * nq)
            q2 = _split_q(q[:, cols])
            s = _dot_nt(q2, kn_ref[0, :, cols].astype(BF16)) + bn_ref[h, 0]
            _online_update(s, vn_ref[0, :, cols].astype(BF16), m_ref, l_ref, acc_ref, rows)
            o_ref[0, :, cols] = _diff_finish(acc_ref[rows], l_ref[rows], lam, sw_ref[...], t, lam_init)


def _sample_attention(q, k_new, v_new, cache_k, cache_v, page_table, rel_bias,
                      lq1, lk1, lq2, lk2, subln_w, lam_init, pps):
    bs, t, _ = q.shape
    ck, cv = cache_k, cache_v
    page = ck.shape[4]
    assert page == LANES
    n_pages = page_table.shape[1]
    past = n_pages * page
    nj = n_pages // pps
    chunk = pps * page
    tq = np.tile(np.arange(t), 2)[:, None]
    kc = np.arange(chunk)[None, :]
    last = _t5_bucket_np(past + tq - (past - chunk + kc))
    assert (_t5_bucket_np(past - (past - chunk) + 1 + np.arange(4 * MAX_DISTANCE)) == NUM_BUCKETS - 1).all()
    far = np.full_like(last, NUM_BUCKETS - 1)
    kn = np.arange(page)[None, :]
    newb = np.where(kn < t, _t5_bucket_np(tq - kn), -1).astype(np.int32)
    bt = _bias_tiles(rel_bias, np.stack([far, last]))
    bn = _bias_tiles(rel_bias, newb[None])
    knp = jnp.pad(k_new, ((0, 0), (0, page - t), (0, 0)))
    vnp = jnp.pad(v_new, ((0, 0), (0, page - t), (0, 0)))
    vec = lambda n: pl.BlockSpec((1, n), lambda b, j, pt: (0, 0))

    def kpage_spec(r):
        return pl.BlockSpec((1,) + ck.shape[1:], lambda b, j, pt: (pt[b, j * pps + r], 0, 0, 0, 0))

    def vpage_spec(r):
        return pl.BlockSpec((page * N_HEADS, V_DIM), lambda b, j, pt: (pt[b, j * pps + r], 0))

    grid_spec = pltpu.PrefetchScalarGridSpec(
        num_scalar_prefetch=1,
        grid=(bs, nj),
        in_specs=[pl.BlockSpec((1, t, QK_WIDTH), lambda b, j, pt: (b, 0, 0)),
                  pl.BlockSpec((1, page, QK_WIDTH), lambda b, j, pt: (b, 0, 0)),
                  pl.BlockSpec((1, page, ATT_WIDTH), lambda b, j, pt: (b, 0, 0)),
                  pl.BlockSpec((N_HEADS, 1, 2 * t, chunk), lambda b, j, pt: (0, (j == nj - 1).astype(jnp.int32), 0, 0)),
                  pl.BlockSpec((N_HEADS, 1, 2 * t, page), lambda b, j, pt: (0, 0, 0, 0)),
                  vec(HEAD_DIM), vec(HEAD_DIM), vec(HEAD_DIM), vec(HEAD_DIM), vec(V_DIM)]
                 + [kpage_spec(r) for r in range(pps)] + [vpage_spec(r) for r in range(pps)],
        out_specs=pl.BlockSpec((1, t, ATT_WIDTH), lambda b, j, pt: (b, 0, 0)),
        scratch_shapes=[pltpu.VMEM((N_HEADS * 2 * t, 1), F32), pltpu.VMEM((N_HEADS * 2 * t, 1), F32),
                        pltpu.VMEM((N_HEADS * 2 * t, V_DIM), F32)],
    )
    return pl.pallas_call(
        functools.partial(_sattn_kernel, pps=pps, t=t, lam_init=lam_init),
        grid_spec=grid_spec,
        out_shape=jax.ShapeDtypeStruct((bs, t, ATT_WIDTH), F32),
        compiler_params=_cparams(("arbitrary", "arbitrary")),
        name="sample_attention",
    )(page_table, q, knp, vnp, bt, bn, lq1, lk1, lq2, lk2, subln_w, *([ck] * pps), *([cv] * pps))


def _layer_norm(y, g, b):
    mu = jnp.mean(y, axis=1, keepdims=True)
    yc = y - mu
    var = jnp.mean(yc * yc, axis=1, keepdims=True)
    return yc * lax.rsqrt(var + LN_EPS) * g + b


def _outproj_kernel(oa_ref, oc_ref, x_ref, mod_ref, w_ref, g_ref, b_ref, x1_ref, h2_ref, *, d, alpha):
    aw = oa_ref.shape[2]
    mix = _dot(oa_ref[0].astype(BF16), w_ref[0:aw, :]) + _dot(oc_ref[0].astype(BF16), w_ref[aw:, :])
    gate1 = mod_ref[0, :, 2 * d:3 * d]
    shift2 = mod_ref[0, :, 3 * d:4 * d]
    scale2 = mod_ref[0, :, 4 * d:5 * d]
    x1 = _layer_norm(alpha * x_ref[0] + gate1 * mix, g_ref[...], b_ref[...])
    x1_ref[0] = x1
    h2_ref[0] = x1 * (1.0 + scale2) + shift2


def _outproj(oa, oc, x, mod, w_out_bf, g, bb, alpha, ts):
    b, s, d = x.shape
    blk = lambda width: pl.BlockSpec((1, ts, width), lambda i, j: (i, j, 0))
    vec = pl.BlockSpec((1, d), lambda i, j: (0, 0))
    return pl.pallas_call(
        functools.partial(_outproj_kernel, d=d, alpha=alpha),
        grid=(b, s // ts),
        in_specs=[blk(oa.shape[2]), blk(oc.shape[2]), blk(d),
                  pl.BlockSpec((1, 1, mod.shape[2]), lambda i, j: (i, 0, 0)),
                  pl.BlockSpec(w_out_bf.shape, lambda i, j: (0, 0)), vec, vec],
        out_specs=[blk(d), blk(d)],
        out_shape=[jax.ShapeDtypeStruct((b, s, d), F32), jax.ShapeDtypeStruct((b, s, d), F32)],
        compiler_params=_cparams(("arbitrary", "arbitrary")),
        name="outproj_ln1",
    )(oa, oc, x, mod, w_out_bf, g, bb)


_CAND_BLOCKS = [(i, 16 if i == 0 else 8, PEER_TOPK // (i + 1)) for i in range(8)]


def _top16_rows(x, iota_n, n, v_ref, i_ref):
    for r in range(PEER_TOPK):
        m = jnp.max(x, axis=0, keepdims=True)
        idx = jnp.min(jnp.where(x == m, iota_n, n), axis=0, keepdims=True)
        v_ref[r:r + 1, :] = m
        i_ref[r:r + 1, :] = idx
        x = jnp.where(iota_n == idx, -jnp.inf, x)


def _route_kernel(h_ref, w_ref, sk_ref, e_ref, g_ref, qt_ref, sv_ref, si_ref, cv_ref, ce_ref, et_ref, gt_ref,
                  *, tb):
    nlb = tb // LANES
    hb = h_ref[...].astype(BF16)
    qt_ref[...] = _dot_nt(w_ref[...], hb)
    iota_n = lax.broadcasted_iota(jnp.int32, (PEER_N_KEYS, LANES), 0)
    half = sk_ref.shape[2]

    def stage1(hp, carry):
        row0 = pl.multiple_of(hp * half, half)
        st = _dot(sk_ref[hp].astype(BF16), qt_ref[pl.ds(row0, half), :].astype(BF16))
        for lb in range(nlb):
            _top16_rows(st[:, lb * LANES:(lb + 1) * LANES], iota_n, PEER_N_KEYS,
                        sv_ref.at[hp, :, lb * LANES:(lb + 1) * LANES],
                        si_ref.at[hp, :, lb * LANES:(lb + 1) * LANES])
        return carry

    lax.fori_loop(0, 2 * PEER_HEADS, stage1, 0)

    sub8 = lax.broadcasted_iota(jnp.int32, (8, LANES), 0)
    sub16 = lax.broadcasted_iota(jnp.int32, (16, LANES), 0)

    def stage2(h, carry):
        for lb in range(nlb):
            ls = slice(lb * LANES, (lb + 1) * LANES)
            sv0 = sv_ref[2 * h, :, ls]
            sv1 = sv_ref[2 * h + 1, :, ls]
            si0 = si_ref[2 * h, :, ls]
            si1 = si_ref[2 * h + 1, :, ls]
            vals, flats, es = [], [], []
            for i, rows, cnt in _CAND_BLOCKS:
                sub = sub16 if rows == 16 else sub8
                vals.append(jnp.where(sub < cnt, sv0[i:i + 1] + sv1[0:rows], -jnp.inf))
                flats.append(i * PEER_TOPK + sub)
                es.append(si0[i:i + 1] * PEER_N_KEYS + si1[0:rows])
            vals.append(sv0[8:16] + sv1[0:1])
            flats.append((sub8 + 8) * PEER_TOPK)
            es.append(si0[8:16] * PEER_N_KEYS + si1[0:1])
            cand = jnp.concatenate(vals, axis=0)
            flat = jnp.concatenate(flats, axis=0)
            ecand = jnp.concatenate(es, axis=0)
            big = PEER_TOPK * PEER_TOPK
            for r in range(PEER_TOPK):
                m = jnp.max(cand, axis=0, keepdims=True)
                fl = jnp.min(jnp.where(cand == m, flat, big), axis=0, keepdims=True)
                hit = flat == fl
                cv_ref[r:r + 1, :] = m
                ce_ref[r:r + 1, :] = jnp.max(jnp.where(hit, ecand, -1), axis=0, keepdims=True)
                cand = jnp.where(hit, -jnp.inf, cand)
            cv = cv_ref[...]
            ex = jnp.exp(cv - cv[0:1])
            g = ex / jnp.sum(ex, axis=0, keepdims=True)
            row0 = pl.multiple_of(h * PEER_TOPK, PEER_TOPK)
            gt_ref[pl.ds(row0, PEER_TOPK), ls] = g
            et_ref[pl.ds(row0, PEER_TOPK), ls] = ce_ref[...]
        return carry

    lax.fori_loop(0, PEER_HEADS, stage2, 0)
    e_ref[...] = et_ref[...].T
    g_ref[...] = gt_ref[...].T


def _peer_route(h2, w_pq_t_bf, sub_keys, tb):
    t, d = h2.shape
    qw = w_pq_t_bf.shape[0]
    hp, n_keys, half = sub_keys.shape
    slots = PEER_HEADS * PEER_TOPK
    return pl.pallas_call(
        functools.partial(_route_kernel, tb=tb),
        grid=(t // tb,),
        in_specs=[pl.BlockSpec((tb, d), lambda i: (i, 0)),
                  pl.BlockSpec((qw, d), lambda i: (0, 0)),
                  pl.BlockSpec((hp, n_keys, half), lambda i: (0, 0, 0))],
        out_specs=[pl.BlockSpec((tb, slots), lambda i: (i, 0)), pl.BlockSpec((tb, slots), lambda i: (i, 0))],
        out_shape=[jax.ShapeDtypeStruct((t, slots), jnp.int32), jax.ShapeDtypeStruct((t, slots), F32)],
        scratch_shapes=[pltpu.VMEM((qw, tb), F32),
                        pltpu.VMEM((hp, PEER_TOPK, tb), F32), pltpu.VMEM((hp, PEER_TOPK, tb), jnp.int32),
                        pltpu.VMEM((PEER_TOPK, LANES), F32), pltpu.VMEM((PEER_TOPK, LANES), jnp.int32),
                        pltpu.VMEM((slots, tb), jnp.int32), pltpu.VMEM((slots, tb), F32)],
        compiler_params=_cparams(("arbitrary",)),
        name="peer_route",
    )(h2, w_pq_t_bf, sub_keys)


def _expand_kernel(e_ref, g_ref, w_ref, *, tb):
    iota = lax.broadcasted_iota(jnp.int32, (PEER_N_KEYS, e_ref.shape[1]), 0)

    def body(t, carry):
        er = e_ref[pl.ds(t, 1), :]
        gr = g_ref[pl.ds(t, 1), :]
        ea = er // PEER_N_KEYS
        eb = er - ea * PEER_N_KEYS
        oa = jnp.where(iota == ea, gr, 0.0).astype(BF16)
        ob = jnp.where(iota == eb, 1.0, 0.0).astype(BF16)
        w_ref[t] = _dot_nt(oa, ob)
        return carry

    lax.fori_loop(0, tb, body, 0, unroll=16)


def _peer_expand(e, g, tb):
    t, slots = e.shape
    return pl.pallas_call(
        functools.partial(_expand_kernel, tb=tb),
        grid=(t // tb,),
        in_specs=[pl.BlockSpec((tb, slots), lambda i: (i, 0)), pl.BlockSpec((tb, slots), lambda i: (i, 0))],
        out_specs=pl.BlockSpec((tb, PEER_N_KEYS, PEER_N_KEYS), lambda i: (i, 0, 0)),
        out_shape=jax.ShapeDtypeStruct((t, PEER_N_KEYS, PEER_N_KEYS), F32),
        compiler_params=_cparams(("arbitrary",)),
        name="peer_expand",
    )(e, g)


def _dense_kernel(h_ref, ut_ref, v_ref, w_ref, f_ref, hb_ref, *, eb, ec):
    j = pl.program_id(1)

    @pl.when(j == 0)
    def _():
        hb_ref[...] = h_ref[...].astype(BF16)
        f_ref[...] = jnp.zeros(f_ref.shape, F32)

    hb = hb_ref[...]
    nc = eb // ec
    a_next = _dot(hb, ut_ref[:, 0:ec])
    for c in range(nc):
        a = a_next
        if c + 1 < nc:
            a_next = _dot(hb, ut_ref[:, (c + 1) * ec:(c + 2) * ec])
        a = 0.5 * a * (1.0 + lax.erf(a * (2.0 ** -0.5)))
        parts = []
        for al in range(ec // PEER_N_KEYS):
            ai = c * (ec // PEER_N_KEYS) + al
            parts.append((a[:, al * PEER_N_KEYS:(al + 1) * PEER_N_KEYS] * w_ref[:, ai, :]).astype(BF16))
        wa = jnp.concatenate(parts, axis=1)
        f_ref[...] += _dot(wa, v_ref[c * ec:(c + 1) * ec, :])


def _peer_dense(h2, u_t_bf, v_bf, w, tb, eb, ec):
    t, d = h2.shape
    n_exp = v_bf.shape[0]
    return pl.pallas_call(
        functools.partial(_dense_kernel, eb=eb, ec=ec),
        grid=(t // tb, n_exp // eb),
        in_specs=[pl.BlockSpec((tb, d), lambda i, j: (i, 0)),
                  pl.BlockSpec((d, eb), lambda i, j: (0, j)),
                  pl.BlockSpec((eb, d), lambda i, j: (j, 0)),
                  pl.BlockSpec((tb, eb // PEER_N_KEYS, PEER_N_KEYS), lambda i, j: (i, j, 0))],
        out_specs=pl.BlockSpec((tb, d), lambda i, j: (i, 0)),
        out_shape=jax.ShapeDtypeStruct((t, d), F32),
        scratch_shapes=[pltpu.VMEM((tb, d), BF16)],
        compiler_params=_cparams(("arbitrary", "arbitrary")),
        name="peer_dense",
    )(h2, u_t_bf, v_bf, w)


def _final_kernel(x1_ref, f_ref, mod_ref, g_ref, b_ref, y_ref, *, d, alpha):
    gate2 = mod_ref[0, :, 5 * d:6 * d]
    y_ref[0] = _layer_norm(alpha * x1_ref[0] + gate2 * f_ref[0], g_ref[...], b_ref[...])


def _final_ln(x1, f, mod, g, bb, alpha, ts):
    b, s, d = x1.shape
    blk = pl.BlockSpec((1, ts, d), lambda i, j: (i, j, 0))
    vec = pl.BlockSpec((1, d), lambda i, j: (0, 0))
    return pl.pallas_call(
        functools.partial(_final_kernel, d=d, alpha=alpha),
        grid=(b, s // ts),
        in_specs=[blk, blk, pl.BlockSpec((1, 1, mod.shape[2]), lambda i, j: (i, 0, 0)), vec, vec],
        out_specs=blk,
        out_shape=jax.ShapeDtypeStruct((b, s, d), F32),
        compiler_params=_cparams(("arbitrary", "arbitrary")),
        name="final_ln2",
    )(x1, f, mod, g, bb)


def _pick(n, prefs):
    for p in prefs:
        if n % p == 0:
            return p
    return n


def _decoder_layer(x, mod, prefix, attend, p, alpha):
    b, s, d = x.shape
    ts = _pick(s, (512, 256, 128))
    q, k, v, oc, conv_state = _inproj(x, mod, prefix, p["w_in"], p["conv_w"], ts)
    oa = attend(q, k, v)
    x1, h2 = _outproj(oa, oc, x, mod, p["w_out"], p["ln1_g"], p["ln1_b"], alpha, ts)
    t = b * s
    h2f = h2.reshape(t, d)
    e, g = _peer_route(h2f, p["w_pq_t"], p["sub_keys"], _pick(t, (512, 256, 128)))
    w = _peer_expand(e, g, _pick(t, (128,)))
    n_exp = p["peer_v"].shape[0]
    f = _peer_dense(h2f, p["peer_u_t"], p["peer_v"], w, _pick(t, (512, 256, 128)),
                    _pick(n_exp, (2048,)), 1024)
    y = _final_ln(x1, f.reshape(b, s, d), mod, p["ln2_g"], p["ln2_b"], alpha, ts)
    return y, k, v, conv_state


def kernel(x_prompt, x_sample, cache_k, cache_v, state_conv, page_table, c_prompt, c_sample, rel_bias,
           w_ada, b_ada, w_in, lambda_q1, lambda_k1, lambda_q2, lambda_k2, subln_w, conv_w, w_out,
           ln1_g, ln1_b, w_pq, sub_keys, peer_u, peer_v, ln2_g, ln2_b):
    depth = w_ada.shape[0]
    bp, sp, d = x_prompt.shape
    bs, tsmp, _ = x_sample.shape
    alpha = (2.0 * depth) ** 0.25
    y_p, y_s = x_prompt, x_sample
    outs = [[] for _ in range(6)]
    zero_prefix = jnp.zeros((bp, CONV_K - 1, conv_w.shape[2]), x_prompt.dtype)
    c_all = jnp.concatenate([c_prompt, c_sample], axis=0)
    n_pool, page = cache_k.shape[1], cache_k.shape[2]
    ck_all = jnp.transpose(cache_k, (0, 1, 3, 4, 5, 2)).reshape(depth * n_pool, N_HEADS, 2, HEAD_DIM, page)
    cv_all = cache_v.reshape(depth * n_pool * page * N_HEADS, V_DIM)
    row = lambda a: a.reshape(1, -1)
    for layer in range(depth):
        lam_init = 0.8 - 0.6 * math.exp(-0.3 * layer)
        p = dict(w_in=w_in[layer].astype(BF16), conv_w=conv_w[layer], w_out=w_out[layer].astype(BF16),
                 ln1_g=row(ln1_g[layer]), ln1_b=row(ln1_b[layer]),
                 w_pq_t=w_pq[layer].T.astype(BF16),
                 sub_keys=sub_keys[layer].reshape(2 * PEER_HEADS, PEER_N_KEYS, -1),
                 peer_u_t=peer_u[layer].T.astype(BF16), peer_v=peer_v[layer].astype(BF16),
                 ln2_g=row(ln2_g[layer]), ln2_b=row(ln2_b[layer]))
        lam_args = (row(lambda_q1[layer]), row(lambda_k1[layer]), row(lambda_q2[layer]),
                    row(lambda_k2[layer]), row(subln_w[layer]))
        mod = _adaln(c_all, w_ada[layer], b_ada[layer])[:, None, :]
        attend_p = lambda q, k, v: _prompt_attention(q, k, v, rel_bias, *lam_args, lam_init, 256, 2)
        pt_layer = page_table + layer * n_pool
        attend_s = lambda q, k, v: _sample_attention(q, k, v, ck_all, cv_all, pt_layer,
                                                     rel_bias, *lam_args, lam_init,
                                                     _pick(page_table.shape[1], (8, 4, 2)))
        y_p, kp, vp, cp = _decoder_layer(y_p, mod[:bp], zero_prefix, attend_p, p, alpha)
        y_s, ks, vs, cs = _decoder_layer(y_s, mod[bp:], state_conv[layer], attend_s, p, alpha)
        for lst, val in zip(outs, (kp.reshape(bp, sp, N_HEADS, 2, HEAD_DIM), vp.reshape(bp, sp, N_HEADS, V_DIM), cp,
                                   ks.reshape(bs, tsmp, N_HEADS, 2, HEAD_DIM), vs.reshape(bs, tsmp, N_HEADS, V_DIM), cs)):
            lst.append(val)
    return (y_p, y_s) + tuple(jnp.stack(o) for o in outs)
```

```python
import functools
import math

import numpy as np
import jax
import jax.numpy as jnp
from jax import lax
from jax.experimental import pallas as pl
from jax.experimental.pallas import tpu as pltpu

N_HEADS = 4
HEAD_DIM = 64
V_DIM = 2 * HEAD_DIM
ATT_WIDTH = N_HEADS * V_DIM
QK_WIDTH = N_HEADS * 2 * HEAD_DIM
CONV_K = 3
NUM_BUCKETS = 32
MAX_DISTANCE = 128
PEER_HEADS = 8
PEER_N_KEYS = 128
PEER_TOPK = 16
LN_EPS = 1e-5

LANES = 128
SUBLANES = 8
VMEM_LIMIT_BYTES = 56 * 1024 * 1024

NEG = -1e30
BF16 = jnp.bfloat16
F32 = jnp.float32


def _cparams(sem):
    return pltpu.CompilerParams(dimension_semantics=sem, vmem_limit_bytes=VMEM_LIMIT_BYTES)


def _dot(a, b):
    return jnp.dot(a, b, preferred_element_type=F32)


def _dot_nt(a, b):
    return lax.dot_general(a, b, (((1,), (1,)), ((), ())), preferred_element_type=F32)


def _adaln_kernel(c_ref, w_ref, b_ref, o_ref):
    c = c_ref[...]
    s = c * jax.nn.sigmoid(c)
    o_ref[...] = _dot(s.astype(BF16), w_ref[...].astype(BF16)) + b_ref[...]


def _adaln(c, w_ada, b_ada):
    n, d = c.shape
    width = w_ada.shape[1]
    tn = width // 4
    return pl.pallas_call(
        _adaln_kernel,
        grid=(width // tn,),
        in_specs=[pl.BlockSpec((n, d), lambda j: (0, 0)),
                  pl.BlockSpec((d, tn), lambda j: (0, j)),
                  pl.BlockSpec((1, tn), lambda j: (0, j))],
        out_specs=pl.BlockSpec((n, tn), lambda j: (0, j)),
        out_shape=jax.ShapeDtypeStruct((n, width), F32),
        compiler_params=_cparams(("arbitrary",)),
        name="adaln",
    )(c, w_ada, b_ada.reshape(1, width))


def _inproj_kernel(x_ref, mod_ref, pre_ref, w_ref, cw_ref, q_ref, k_ref, v_ref, oc_ref, cs_ref, zbuf,
                   *, ts, d, cw):
    @pl.when(pl.program_id(1) == 0)
    def _():
        zbuf[SUBLANES - 2:SUBLANES, :] = pre_ref[0]

    shift1 = mod_ref[0, :, 0:d]
    scale1 = mod_ref[0, :, d:2 * d]
    h = x_ref[0] * (1.0 + scale1) + shift1
    proj = _dot(h.astype(BF16), w_ref[...])
    o1 = 2 * QK_WIDTH
    o2 = o1 + ATT_WIDTH
    q_ref[0] = proj[:, :QK_WIDTH]
    k_ref[0] = proj[:, QK_WIDTH:o1]
    v_ref[0] = proj[:, o1:o2]
    gb = proj[:, o2:o2 + cw]
    z = proj[:, o2 + cw:o2 + 2 * cw] * proj[:, o2 + 2 * cw:o2 + 3 * cw]
    zbuf[SUBLANES:SUBLANES + ts, :] = z
    y = (cw_ref[0:1, :] * zbuf[SUBLANES - 2:SUBLANES - 2 + ts, :]
         + cw_ref[1:2, :] * zbuf[SUBLANES - 1:SUBLANES - 1 + ts, :]
         + cw_ref[2:3, :] * z)
    oc_ref[0] = gb * y
    tail = zbuf[SUBLANES + ts - 2:SUBLANES + ts, :]
    cs_ref[0] = tail
    zbuf[SUBLANES - 2:SUBLANES, :] = tail


def _inproj(x, mod, prefix, w_in_bf, conv_w, ts):
    b, s, d = x.shape
    cw = conv_w.shape[1]
    pw = w_in_bf.shape[1]
    blk = lambda width: pl.BlockSpec((1, ts, width), lambda i, j: (i, j, 0))
    outs = pl.pallas_call(
        functools.partial(_inproj_kernel, ts=ts, d=d, cw=cw),
        grid=(b, s // ts),
        in_specs=[blk(d),
                  pl.BlockSpec((1, 1, mod.shape[2]), lambda i, j: (i, 0, 0)),
                  pl.BlockSpec((1, CONV_K - 1, cw), lambda i, j: (i, 0, 0)),
                  pl.BlockSpec((d, pw), lambda i, j: (0, 0)),
                  pl.BlockSpec((CONV_K, cw), lambda i, j: (0, 0))],
        out_specs=[blk(QK_WIDTH), blk(QK_WIDTH), blk(ATT_WIDTH), blk(cw),
                   pl.BlockSpec((1, CONV_K - 1, cw), lambda i, j: (i, 0, 0))],
        out_shape=[jax.ShapeDtypeStruct((b, s, QK_WIDTH), F32),
                   jax.ShapeDtypeStruct((b, s, QK_WIDTH), F32),
                   jax.ShapeDtypeStruct((b, s, ATT_WIDTH), F32),
                   jax.ShapeDtypeStruct((b, s, cw), F32),
                   jax.ShapeDtypeStruct((b, CONV_K - 1, cw), F32)],
        scratch_shapes=[pltpu.VMEM((SUBLANES + ts, cw), F32)],
        compiler_params=_cparams(("arbitrary", "arbitrary")),
        name="inproj",
    )(x, mod, prefix, w_in_bf, conv_w)
    return outs


def _t5_bucket_np(dist):
    dist = np.asarray(dist, np.int64)
    n = np.maximum(dist, 0)
    max_exact = NUM_BUCKETS // 2
    n_large = np.maximum(n, max_exact).astype(np.float64)
    large = max_exact + (np.log(n_large / max_exact) / math.log(MAX_DISTANCE / max_exact)
                         * (NUM_BUCKETS - max_exact)).astype(np.int64)
    large = np.minimum(large, NUM_BUCKETS - 1)
    bucket = np.where(n < max_exact, n, large)
    return np.where(dist < 0, -1, bucket).astype(np.int32)


def _bias_kernel(rb_ref, bk_ref, o_ref, *, rel_to):
    h = pl.program_id(0)
    bk = bk_ref[0]
    base = 0.0 if rel_to is None else rb_ref[rel_to, h]
    acc = jnp.where(bk < 0, NEG, 0.0).astype(F32)
    for b in range(NUM_BUCKETS):
        acc = jnp.where(bk == b, rb_ref[b, h] - base, acc)
    o_ref[0, 0] = acc


def _bias_tiles(rel_bias, buckets, rel_to=None):
    n, r, c = buckets.shape
    return pl.pallas_call(
        functools.partial(_bias_kernel, rel_to=rel_to),
        grid=(N_HEADS, n),
        in_specs=[pl.BlockSpec(memory_space=pltpu.SMEM),
                  pl.BlockSpec((1, r, c), lambda h, i: (i, 0, 0))],
        out_specs=pl.BlockSpec((1, 1, r, c), lambda h, i: (h, i, 0, 0)),
        out_shape=jax.ShapeDtypeStruct((N_HEADS, n, r, c), F32),
        compiler_params=_cparams(("arbitrary", "arbitrary")),
        name="bias_tiles",
    )(rel_bias, jnp.asarray(buckets))


def _far_bucket_from(dist_lo):
    b = _t5_bucket_np(np.arange(dist_lo, dist_lo + 4 * MAX_DISTANCE))
    assert (b == NUM_BUCKETS - 1).all()
    return NUM_BUCKETS - 1


def _diff_lambda(lq1_ref, lk1_ref, lq2_ref, lk2_ref, lam_init):
    a = jnp.sum(lq1_ref[...] * lk1_ref[...], axis=1, keepdims=True)
    b = jnp.sum(lq2_ref[...] * lk2_ref[...], axis=1, keepdims=True)
    return jnp.exp(a) - jnp.exp(b) + lam_init


def _split_q(q):
    lane = lax.broadcasted_iota(jnp.int32, q.shape, 1)
    q1 = jnp.where(lane < HEAD_DIM, q, 0.0)
    q2 = jnp.where(lane >= HEAD_DIM, q, 0.0)
    return jnp.concatenate([q1, q2], axis=0).astype(BF16)


def _online_update(s, v_bf, m_ref, l_ref, acc_ref, rows):
    m_old = m_ref[rows]
    m_new = jnp.maximum(m_old, jnp.max(s, axis=1, keepdims=True))
    p = jnp.exp(s - m_new)
    alpha = jnp.exp(m_old - m_new)
    l_ref[rows] = alpha * l_ref[rows] + jnp.sum(p, axis=1, keepdims=True)
    acc_ref[rows] = alpha * acc_ref[rows] + _dot(p.astype(BF16), v_bf)
    m_ref[rows] = m_new


def _diff_finish(acc, l, lam, sw, nq, lam_init):
    o = acc[:nq] / l[:nq] - lam * (acc[nq:] / l[nq:])
    o = o * lax.rsqrt(jnp.mean(o * o, axis=1, keepdims=True) + LN_EPS)
    return o * sw * (1.0 - lam_init)


def _pattn_kernel(q_ref, k_ref, v_ref, bt_ref, lq1, lk1, lq2, lk2, sw_ref, o_ref,
                  kb_ref, vt_ref, m_ref, l_ref, acc_ref, *, qb, nh, lam_init):
    i = pl.program_id(2)
    n_kt = kb_ref.shape[1]

    @pl.when(i == 0)
    def _():
        for g in range(nh):
            cols = slice(g * V_DIM, (g + 1) * V_DIM)
            for c in range(n_kt):
                kb_ref[g, c] = k_ref[0, c * qb:(c + 1) * qb, cols].astype(BF16)
                vt_ref[g, c] = v_ref[0, c * qb:(c + 1) * qb, cols].T.astype(BF16)

    q2 = [_split_q(q_ref[0, :, g * V_DIM:(g + 1) * V_DIM] * (HEAD_DIM ** -0.5)) for g in range(nh)]
    m_ref[...] = jnp.full(m_ref.shape, NEG, F32)
    l_ref[...] = jnp.zeros(l_ref.shape, F32)
    acc_ref[...] = jnp.zeros(acc_ref.shape, F32)

    def tiles(items):
        old = [(m_ref[g], l_ref[g], acc_ref[g]) for g in range(nh)]
        new = []
        for g in range(nh):
            sts = []
            for j, slot in items:
                st = _dot_nt(kb_ref[g, j], q2[g])
                sts.append(st if slot is None else st + bt_ref[g, slot])
            m_old, l_old, acc_old = old[g]
            m_new = m_old
            for st in sts:
                m_new = jnp.maximum(m_new, jnp.max(st, axis=0, keepdims=True))
            alpha = jnp.exp(m_old - m_new)
            l_new = alpha * l_old
            acc_new = alpha * acc_old
            for (j, _), st in zip(items, sts):
                p = jnp.exp(st - m_new)
                l_new = l_new + jnp.sum(p, axis=0, keepdims=True)
                acc_new = acc_new + _dot(vt_ref[g, j], p.astype(BF16))
            new.append((m_new, l_new, acc_new))
        for g in range(nh):
            m_ref[g], l_ref[g], acc_ref[g] = new[g]

    n_far = jnp.maximum(i - 1, 0)

    n_quad = n_far // 4

    def far_quad(jj, carry):
        tiles([(4 * jj + r, None) for r in range(4)])
        return carry

    lax.fori_loop(0, n_quad, far_quad, 0)

    @pl.when(n_far % 4 >= 2)
    def _():
        tiles([(4 * n_quad, None), (4 * n_quad + 1, None)])

    @pl.when(n_far % 2 == 1)
    def _():
        tiles([(n_far - 1, None)])

    @pl.when(i >= 1)
    def _():
        tiles([(i - 1, 1), (i, 0)])

    @pl.when(i == 0)
    def _():
        tiles([(i, 0)])

    lam = _diff_lambda(lq1, lk1, lq2, lk2, lam_init)
    for g in range(nh):
        acc = acc_ref[g]
        l = l_ref[g]
        ot = acc[:, :qb] / l[:, :qb] - lam * (acc[:, qb:] / l[:, qb:])
        ot = ot * lax.rsqrt(jnp.mean(ot * ot, axis=0, keepdims=True) + LN_EPS)
        o_ref[0, :, g * V_DIM:(g + 1) * V_DIM] = ot.T * sw_ref[...] * (1.0 - lam_init)


def _prompt_attention(q, k, v, rel_bias, lq1, lk1, lq2, lk2, subln_w, lam_init, qb, nh):
    b, s, _ = q.shape
    kk = np.arange(qb)[:, None]
    qq = np.tile(np.arange(qb), 2)[None, :]
    buckets = np.stack([_t5_bucket_np(qq - kk), _t5_bucket_np(qb + qq - kk)])
    far_bucket = _far_bucket_from(qb + 1)
    bt = _bias_tiles(rel_bias, buckets, rel_to=far_bucket)
    vec = lambda n: pl.BlockSpec((1, n), lambda bi, h, i: (0, 0))
    gw = nh * V_DIM
    n_kt = s // qb
    return pl.pallas_call(
        functools.partial(_pattn_kernel, qb=qb, nh=nh, lam_init=lam_init),
        grid=(b, N_HEADS // nh, n_kt),
        in_specs=[pl.BlockSpec((1, qb, gw), lambda bi, h, i: (bi, i, h)),
                  pl.BlockSpec((1, s, gw), lambda bi, h, i: (bi, 0, h)),
                  pl.BlockSpec((1, s, gw), lambda bi, h, i: (bi, 0, h)),
                  pl.BlockSpec((nh, 2, qb, 2 * qb), lambda bi, h, i: (h, 0, 0, 0)),
                  vec(HEAD_DIM), vec(HEAD_DIM), vec(HEAD_DIM), vec(HEAD_DIM), vec(V_DIM)],
        out_specs=pl.BlockSpec((1, qb, gw), lambda bi, h, i: (bi, i, h)),
        out_shape=jax.ShapeDtypeStruct((b, s, ATT_WIDTH), F32),
        scratch_shapes=[pltpu.VMEM((nh, n_kt, qb, V_DIM), BF16), pltpu.VMEM((nh, n_kt, V_DIM, qb), BF16),
                        pltpu.VMEM((nh, 1, 2 * qb), F32), pltpu.VMEM((nh, 1, 2 * qb), F32),
                        pltpu.VMEM((nh, V_DIM, 2 * qb), F32)],
        compiler_params=_cparams(("arbitrary", "arbitrary", "arbitrary")),
        name="prompt_attention",
    )(q, k, v, bt, lq1, lk1, lq2, lk2, subln_w)


def _sattn_kernel(pt_ref, q_ref, kn_ref, vn_ref, bt_ref, bn_ref, lq1, lk1, lq2, lk2, sw_ref, *rest,
                  pps, t, lam_init):
    kp = rest[:pps]
    vp = rest[pps:2 * pps]
    o_ref = rest[2 * pps]
    m_ref, l_ref, acc_ref = rest[2 * pps + 1:]
    j = pl.program_id(1)
    nj = pl.num_programs(1)
    nq = 2 * t

    @pl.when(j == 0)
    def _():
        m_ref[...] = jnp.full(m_ref.shape, NEG, F32)
        l_ref[...] = jnp.zeros(l_ref.shape, F32)
        acc_ref[...] = jnp.zeros(acc_ref.shape, F32)

    q = q_ref[0] * (HEAD_DIM ** -0.5)
    m_all, l_all, acc_all = m_ref[...], l_ref[...], acc_ref[...]
    new_m, new_l, new_acc = [], [], []
    for h in range(N_HEADS):
        cols = slice(h * V_DIM, (h + 1) * V_DIM)
        rows = slice(h * nq, (h + 1) * nq)
        q2 = _split_q(q[:, cols])
        s = jnp.concatenate([_dot(q2, kp[r][0, h].reshape(V_DIM, -1).astype(BF16)) for r in range(pps)],
                            axis=1)
        s = s + bt_ref[h, 0]
        m_old = m_all[rows]
        m_new = jnp.maximum(m_old, jnp.max(s, axis=1, keepdims=True))
        p = jnp.exp(s - m_new)
        alpha = jnp.exp(m_old - m_new)
        new_l.append(alpha * l_all[rows] + jnp.sum(p, axis=1, keepdims=True))
        p = p.astype(BF16)
        vrow = pl.ds(h, LANES, stride=N_HEADS)
        pv = _dot(p[:, 0:LANES], vp[0][vrow, :].astype(BF16))
        for r in range(1, pps):
            pv = pv + _dot(p[:, r * LANES:(r + 1) * LANES], vp[r][vrow, :].astype(BF16))
        new_acc.append(alpha * acc_all[rows] + pv)
        new_m.append(m_new)
    m_ref[...] = jnp.concatenate(new_m, axis=0)
    l_ref[...] = jnp.concatenate(new_l, axis=0)
    acc_ref[...] = jnp.concatenate(new_acc, axis=0)

    @pl.when(j == nj - 1)
    def _():
        lam = _diff_lambda(lq1, lk1, lq2, lk2, lam_init)
        for h in range(N_HEADS):
            cols = slice(h * V_DIM, (h + 1) * V_DIM)
            rows = slice(h * nq, (h + 1) * nq)
            q2 = _split_q(q[:, cols])
            s = _dot_nt(q2, kn_ref[0, :, cols].astype(BF16)) + bn_ref[h, 0]
            _online_update(s, vn_ref[0, :, cols].astype(BF16), m_ref, l_ref, acc_ref, rows)
            o_ref[0, :, cols] = _diff_finish(acc_ref[rows], l_ref[rows], lam, sw_ref[...], t, lam_init)


def _sample_attention(q, k_new, v_new, cache_k, cache_v, page_table, rel_bias,
                      lq1, lk1, lq2, lk2, subln_w, lam_init, pps):
    bs, t, _ = q.shape
    ck, cv = cache_k, cache_v
    page = ck.shape[4]
    assert page == LANES
    n_pages = page_table.shape[1]
    past = n_pages * page
    nj = n_pages // pps
    chunk = pps * page
    tq = np.tile(np.arange(t), 2)[:, None]
    kc = np.arange(chunk)[None, :]
    last = _t5_bucket_np(past + tq - (past - chunk + kc))
    assert (_t5_bucket_np(past - (past - chunk) + 1 + np.arange(4 * MAX_DISTANCE)) == NUM_BUCKETS - 1).all()
    far = np.full_like(last, NUM_BUCKETS - 1)
    kn = np.arange(page)[None, :]
    newb = np.where(kn < t, _t5_bucket_np(tq - kn), -1).astype(np.int32)
    bt = _bias_tiles(rel_bias, np.stack([far, last]))
    bn = _bias_tiles(rel_bias, newb[None])
    knp = jnp.pad(k_new, ((0, 0), (0, page - t), (0, 0)))
    vnp = jnp.pad(v_new, ((0, 0), (0, page - t), (0, 0)))
    vec = lambda n: pl.BlockSpec((1, n), lambda b, j, pt: (0, 0))

    def kpage_spec(r):
        return pl.BlockSpec((1,) + ck.shape[1:], lambda b, j, pt: (pt[b, j * pps + r], 0, 0, 0, 0))

    def vpage_spec(r):
        return pl.BlockSpec((page * N_HEADS, V_DIM), lambda b, j, pt: (pt[b, j * pps + r], 0))

    grid_spec = pltpu.PrefetchScalarGridSpec(
        num_scalar_prefetch=1,
        grid=(bs, nj),
        in_specs=[pl.BlockSpec((1, t, QK_WIDTH), lambda b, j, pt: (b, 0, 0)),
                  pl.BlockSpec((1, page, QK_WIDTH), lambda b, j, pt: (b, 0, 0)),
                  pl.BlockSpec((1, page, ATT_WIDTH), lambda b, j, pt: (b, 0, 0)),
                  pl.BlockSpec((N_HEADS, 1, 2 * t, chunk), lambda b, j, pt: (0, (j == nj - 1).astype(jnp.int32), 0, 0)),
                  pl.BlockSpec((N_HEADS, 1, 2 * t, page), lambda b, j, pt: (0, 0, 0, 0)),
                  vec(HEAD_DIM), vec(HEAD_DIM), vec(HEAD_DIM), vec(HEAD_DIM), vec(V_DIM)]
                 + [kpage_spec(r) for r in range(pps)] + [vpage_spec(r) for r in range(pps)],
        out_specs=pl.BlockSpec((1, t, ATT_WIDTH), lambda b, j, pt: (b, 0, 0)),
        scratch_shapes=[pltpu.VMEM((N_HEADS * 2 * t, 1), F32), pltpu.VMEM((N_HEADS * 2 * t, 1), F32),
                        pltpu.VMEM((N_HEADS * 2 * t, V_DIM), F32)],
    )
    return pl.pallas_call(
        functools.partial(_sattn_kernel, pps=pps, t=t, lam_init=lam_init),
        grid_spec=grid_spec,
        out_shape=jax.ShapeDtypeStruct((bs, t, ATT_WIDTH), F32),
        compiler_params=_cparams(("arbitrary", "arbitrary")),
        name="sample_attention",
    )(page_table, q, knp, vnp, bt, bn, lq1, lk1, lq2, lk2, subln_w, *([ck] * pps), *([cv] * pps))


def _layer_norm(y, g, b):
    mu = jnp.mean(y, axis=1, keepdims=True)
    yc = y - mu
    var = jnp.mean(yc * yc, axis=1, keepdims=True)
    return yc * lax.rsqrt(var + LN_EPS) * g + b


def _outproj_kernel(oa_ref, oc_ref, x_ref, mod_ref, w_ref, g_ref, b_ref, x1_ref, h2_ref, *, d, alpha):
    aw = oa_ref.shape[2]
    mix = _dot(oa_ref[0].astype(BF16), w_ref[0:aw, :]) + _dot(oc_ref[0].astype(BF16), w_ref[aw:, :])
    gate1 = mod_ref[0, :, 2 * d:3 * d]
    shift2 = mod_ref[0, :, 3 * d:4 * d]
    scale2 = mod_ref[0, :, 4 * d:5 * d]
    x1 = _layer_norm(alpha * x_ref[0] + gate1 * mix, g_ref[...], b_ref[...])
    x1_ref[0] = x1
    h2_ref[0] = x1 * (1.0 + scale2) + shift2


def _outproj(oa, oc, x, mod, w_out_bf, g, bb, alpha, ts):
    b, s, d = x.shape
    blk = lambda width: pl.BlockSpec((1, ts, width), lambda i, j: (i, j, 0))
    vec = pl.BlockSpec((1, d), lambda i, j: (0, 0))
    return pl.pallas_call(
        functools.partial(_outproj_kernel, d=d, alpha=alpha),
        grid=(b, s // ts),
        in_specs=[blk(oa.shape[2]), blk(oc.shape[2]), blk(d),
                  pl.BlockSpec((1, 1, mod.shape[2]), lambda i, j: (i, 0, 0)),
                  pl.BlockSpec(w_out_bf.shape, lambda i, j: (0, 0)), vec, vec],
        out_specs=[blk(d), blk(d)],
        out_shape=[jax.ShapeDtypeStruct((b, s, d), F32), jax.ShapeDtypeStruct((b, s, d), F32)],
        compiler_params=_cparams(("arbitrary", "arbitrary")),
        name="outproj_ln1",
    )(oa, oc, x, mod, w_out_bf, g, bb)


_CAND_BLOCKS = [(i, 16 if i == 0 else 8, PEER_TOPK // (i + 1)) for i in range(8)]


def _top16_rows(x, iota_n, n, v_ref, i_ref):
    for r in range(PEER_TOPK):
        m = jnp.max(x, axis=0, keepdims=True)
        idx = jnp.min(jnp.where(x == m, iota_n, n), axis=0, keepdims=True)
        v_ref[r:r + 1, :] = m
        i_ref[r:r + 1, :] = idx
        x = jnp.where(iota_n == idx, -jnp.inf, x)


def _route_kernel(h_ref, w_ref, sk_ref, e_ref, g_ref, qt_ref, sv_ref, si_ref, cv_ref, ce_ref, et_ref, gt_ref,
                  *, tb):
    nlb = tb // LANES
    hb = h_ref[...].astype(BF16)
    qt_ref[...] = _dot_nt(w_ref[...], hb)
    iota_n = lax.broadcasted_iota(jnp.int32, (PEER_N_KEYS, LANES), 0)
    half = sk_ref.shape[2]

    def stage1(hp, carry):
        row0 = pl.multiple_of(hp * half, half)
        st = _dot(sk_ref[hp].astype(BF16), qt_ref[pl.ds(row0, half), :].astype(BF16))
        for lb in range(nlb):
            _top16_rows(st[:, lb * LANES:(lb + 1) * LANES], iota_n, PEER_N_KEYS,
                        sv_ref.at[hp, :, lb * LANES:(lb + 1) * LANES],
                        si_ref.at[hp, :, lb * LANES:(lb + 1) * LANES])
        return carry

    lax.fori_loop(0, 2 * PEER_HEADS, stage1, 0)

    sub8 = lax.broadcasted_iota(jnp.int32, (8, LANES), 0)
    sub16 = lax.broadcasted_iota(jnp.int32, (16, LANES), 0)

    def stage2(h, carry):
        for lb in range(nlb):
            ls = slice(lb * LANES, (lb + 1) * LANES)
            sv0 = sv_ref[2 * h, :, ls]
            sv1 = sv_ref[2 * h + 1, :, ls]
            si0 = si_ref[2 * h, :, ls]
            si1 = si_ref[2 * h + 1, :, ls]
            vals, flats = [], []
            for i, rows, cnt in _CAND_BLOCKS:
                sub = sub16 if rows == 16 else sub8
                vals.append(jnp.where(sub < cnt, sv0[i:i + 1] + sv1[0:rows], -jnp.inf))
                flats.append(i * PEER_TOPK + sub)
            vals.append(sv0[8:16] + sv1[0:1])
            flats.append((sub8 + 8) * PEER_TOPK)
            cand = jnp.concatenate(vals, axis=0)
            flat = jnp.concatenate(flats, axis=0)
            big = PEER_TOPK * PEER_TOPK
            for r in range(PEER_TOPK):
                m = jnp.max(cand, axis=0, keepdims=True)
                fl = jnp.min(jnp.where(cand == m, flat, big), axis=0, keepdims=True)
                cv_ref[r:r + 1, :] = m
                ce_ref[r:r + 1, :] = fl
                cand = jnp.where(flat == fl, -jnp.inf, cand)
            cv = cv_ref[...]
            ex = jnp.exp(cv - cv[0:1])
            g = ex / jnp.sum(ex, axis=0, keepdims=True)
            fl = ce_ref[...]
            fi = fl // PEER_TOPK
            fj = fl - fi * PEER_TOPK
            ea = jnp.zeros_like(fl)
            eb = jnp.zeros_like(fl)
            for i in range(PEER_TOPK):
                ea = jnp.where(fi == i, si0[i:i + 1], ea)
                eb = jnp.where(fj == i, si1[i:i + 1], eb)
            row0 = pl.multiple_of(h * PEER_TOPK, PEER_TOPK)
            gt_ref[pl.ds(row0, PEER_TOPK), ls] = g
            et_ref[pl.ds(row0, PEER_TOPK), ls] = ea * PEER_N_KEYS + eb
        return carry

    lax.fori_loop(0, PEER_HEADS, stage2, 0)
    e_ref[...] = et_ref[...].T
    g_ref[...] = gt_ref[...].T


def _peer_route(h2, w_pq_t_bf, sub_keys, tb):
    t, d = h2.shape
    qw = w_pq_t_bf.shape[0]
    hp, n_keys, half = sub_keys.shape
    slots = PEER_HEADS * PEER_TOPK
    return pl.pallas_call(
        functools.partial(_route_kernel, tb=tb),
        grid=(t // tb,),
        in_specs=[pl.BlockSpec((tb, d), lambda i: (i, 0)),
                  pl.BlockSpec((qw, d), lambda i: (0, 0)),
                  pl.BlockSpec((hp, n_keys, half), lambda i: (0, 0, 0))],
        out_specs=[pl.BlockSpec((tb, slots), lambda i: (i, 0)), pl.BlockSpec((tb, slots), lambda i: (i, 0))],
        out_shape=[jax.ShapeDtypeStruct((t, slots), jnp.int32), jax.ShapeDtypeStruct((t, slots), F32)],
        scratch_shapes=[pltpu.VMEM((qw, tb), F32),
                        pltpu.VMEM((hp, PEER_TOPK, tb), F32), pltpu.VMEM((hp, PEER_TOPK, tb), jnp.int32),
                        pltpu.VMEM((PEER_TOPK, LANES), F32), pltpu.VMEM((PEER_TOPK, LANES), jnp.int32),
                        pltpu.VMEM((slots, tb), jnp.int32), pltpu.VMEM((slots, tb), F32)],
        compiler_params=_cparams(("arbitrary",)),
        name="peer_route",
    )(h2, w_pq_t_bf, sub_keys)


def _expand_kernel(e_ref, g_ref, w_ref, *, tb):
    iota = lax.broadcasted_iota(jnp.int32, (PEER_N_KEYS, e_ref.shape[1]), 0)

    def body(t, carry):
        er = e_ref[pl.ds(t, 1), :]
        gr = g_ref[pl.ds(t, 1), :]
        ea = er // PEER_N_KEYS
        eb = er - ea * PEER_N_KEYS
        oa = jnp.where(iota == ea, gr, 0.0).astype(BF16)
        ob = jnp.where(iota == eb, 1.0, 0.0).astype(BF16)
        w_ref[t] = _dot_nt(oa, ob)
        return carry

    lax.fori_loop(0, tb, body, 0, unroll=32)


def _peer_expand(e, g, tb):
    t, slots = e.shape
    return pl.pallas_call(
        functools.partial(_expand_kernel, tb=tb),
        grid=(t // tb,),
        in_specs=[pl.BlockSpec((tb, slots), lambda i: (i, 0)), pl.BlockSpec((tb, slots), lambda i: (i, 0))],
        out_specs=pl.BlockSpec((tb, PEER_N_KEYS, PEER_N_KEYS), lambda i: (i, 0, 0)),
        out_shape=jax.ShapeDtypeStruct((t, PEER_N_KEYS, PEER_N_KEYS), F32),
        compiler_params=_cparams(("arbitrary",)),
        name="peer_expand",
    )(e, g)


def _dense_kernel(h_ref, ut_ref, v_ref, w_ref, f_ref, hb_ref, *, eb, ec):
    j = pl.program_id(1)

    @pl.when(j == 0)
    def _():
        hb_ref[...] = h_ref[...].astype(BF16)
        f_ref[...] = jnp.zeros(f_ref.shape, F32)

    hb = hb_ref[...]
    nc = eb // ec
    a_next = _dot(hb, ut_ref[:, 0:ec])
    for c in range(nc):
        a = a_next
        if c + 1 < nc:
            a_next = _dot(hb, ut_ref[:, (c + 1) * ec:(c + 2) * ec])
        a = 0.5 * a * (1.0 + lax.erf(a * (2.0 ** -0.5)))
        parts = []
        for al in range(ec // PEER_N_KEYS):
            ai = c * (ec // PEER_N_KEYS) + al
            parts.append((a[:, al * PEER_N_KEYS:(al + 1) * PEER_N_KEYS] * w_ref[:, ai, :]).astype(BF16))
        wa = jnp.concatenate(parts, axis=1)
        f_ref[...] += _dot(wa, v_ref[c * ec:(c + 1) * ec, :])


def _peer_dense(h2, u_t_bf, v_bf, w, tb, eb, ec):
    t, d = h2.shape
    n_exp = v_bf.shape[0]
    return pl.pallas_call(
        functools.partial(_dense_kernel, eb=eb, ec=ec),
        grid=(t // tb, n_exp // eb),
        in_specs=[pl.BlockSpec((tb, d), lambda i, j: (i, 0)),
                  pl.BlockSpec((d, eb), lambda i, j: (0, j)),
                  pl.BlockSpec((eb, d), lambda i, j: (j, 0)),
                  pl.BlockSpec((tb, eb // PEER_N_KEYS, PEER_N_KEYS), lambda i, j: (i, j, 0))],
        out_specs=pl.BlockSpec((tb, d), lambda i, j: (i, 0)),
        out_shape=jax.ShapeDtypeStruct((t, d), F32),
        scratch_shapes=[pltpu.VMEM((tb, d), BF16)],
        compiler_params=_cparams(("arbitrary", "arbitrary")),
        name="peer_dense",
    )(h2, u_t_bf, v_bf, w)


def _final_kernel(x1_ref, f_ref, mod_ref, g_ref, b_ref, y_ref, *, d, alpha):
    gate2 = mod_ref[0, :, 5 * d:6 * d]
    y_ref[0] = _layer_norm(alpha * x1_ref[0] + gate2 * f_ref[0], g_ref[...], b_ref[...])


def _final_ln(x1, f, mod, g, bb, alpha, ts):
    b, s, d = x1.shape
    blk = pl.BlockSpec((1, ts, d), lambda i, j: (i, j, 0))
    vec = pl.BlockSpec((1, d), lambda i, j: (0, 0))
    return pl.pallas_call(
        functools.partial(_final_kernel, d=d, alpha=alpha),
        grid=(b, s // ts),
        in_specs=[blk, blk, pl.BlockSpec((1, 1, mod.shape[2]), lambda i, j: (i, 0, 0)), vec, vec],
        out_specs=blk,
        out_shape=jax.ShapeDtypeStruct((b, s, d), F32),
        compiler_params=_cparams(("arbitrary", "arbitrary")),
        name="final_ln2",
    )(x1, f, mod, g, bb)


def _pick(n, prefs):
    for p in prefs:
        if n % p == 0:
            return p
    return n


def _decoder_layer(x, mod, prefix, attend, p, alpha):
    b, s, d = x.shape
    ts = _pick(s, (512, 256, 128))
    q, k, v, oc, conv_state = _inproj(x, mod, prefix, p["w_in"], p["conv_w"], ts)
    oa = attend(q, k, v)
    x1, h2 = _outproj(oa, oc, x, mod, p["w_out"], p["ln1_g"], p["ln1_b"], alpha, ts)
    t = b * s
    h2f = h2.reshape(t, d)
    e, g = _peer_route(h2f, p["w_pq_t"], p["sub_keys"], _pick(t, (512, 256, 128)))
    w = _peer_expand(e, g, _pick(t, (128,)))
    n_exp = p["peer_v"].shape[0]
    f = _peer_dense(h2f, p["peer_u_t"], p["peer_v"], w, _pick(t, (512, 256, 128)),
                    _pick(n_exp, (2048,)), 1024)
    y = _final_ln(x1, f.reshape(b, s, d), mod, p["ln2_g"], p["ln2_b"], alpha, ts)
    return y, k, v, conv_state


def kernel(x_prompt, x_sample, cache_k, cache_v, state_conv, page_table, c_prompt, c_sample, rel_bias,
           w_ada, b_ada, w_in, lambda_q1, lambda_k1, lambda_q2, lambda_k2, subln_w, conv_w, w_out,
           ln1_g, ln1_b, w_pq, sub_keys, peer_u, peer_v, ln2_g, ln2_b):
    depth = w_ada.shape[0]
    bp, sp, d = x_prompt.shape
    bs, tsmp, _ = x_sample.shape
    alpha = (2.0 * depth) ** 0.25
    y_p, y_s = x_prompt, x_sample
    outs = [[] for _ in range(6)]
    zero_prefix = jnp.zeros((bp, CONV_K - 1, conv_w.shape[2]), x_prompt.dtype)
    c_all = jnp.concatenate([c_prompt, c_sample], axis=0)
    n_pool, page = cache_k.shape[1], cache_k.shape[2]
    ck_all = jnp.transpose(cache_k, (0, 1, 3, 4, 5, 2)).reshape(depth * n_pool, N_HEADS, 2, HEAD_DIM, page)
    cv_all = cache_v.reshape(depth * n_pool * page * N_HEADS, V_DIM)
    row = lambda a: a.reshape(1, -1)
    for layer in range(depth):
        lam_init = 0.8 - 0.6 * math.exp(-0.3 * layer)
        p = dict(w_in=w_in[layer].astype(BF16), conv_w=conv_w[layer], w_out=w_out[layer].astype(BF16),
                 ln1_g=row(ln1_g[layer]), ln1_b=row(ln1_b[layer]),
                 w_pq_t=w_pq[layer].T.astype(BF16),
                 sub_keys=sub_keys[layer].reshape(2 * PEER_HEADS, PEER_N_KEYS, -1),
                 peer_u_t=peer_u[layer].T.astype(BF16), peer_v=peer_v[layer].astype(BF16),
                 ln2_g=row(ln2_g[layer]), ln2_b=row(ln2_b[layer]))
        lam_args = (row(lambda_q1[layer]), row(lambda_k1[layer]), row(lambda_q2[layer]),
                    row(lambda_k2[layer]), row(subln_w[layer]))
        mod = _adaln(c_all, w_ada[layer], b_ada[layer])[:, None, :]
        attend_p = lambda q, k, v: _prompt_attention(q, k, v, rel_bias, *lam_args, lam_init, 256, 2)
        pt_layer = page_table + layer * n_pool
        attend_s = lambda q, k, v: _sample_attention(q, k, v, ck_all, cv_all, pt_layer,
                                                     rel_bias, *lam_args, lam_init,
                                                     _pick(page_table.shape[1], (16, 8, 4, 2)))
        y_p, kp, vp, cp = _decoder_layer(y_p, mod[:bp], zero_prefix, attend_p, p, alpha)
        y_s, ks, vs, cs = _decoder_layer(y_s, mod[bp:], state_conv[layer], attend_s, p, alpha)
        for lst, val in zip(outs, (kp.reshape(bp, sp, N_HEADS, 2, HEAD_DIM), vp.reshape(bp, sp, N_HEADS, V_DIM), cp,
                                   ks.reshape(bs, tsmp, N_HEADS, 2, HEAD_DIM), vs.reshape(bs, tsmp, N_HEADS, V_DIM), cs)):
            lst.append(val)
    return (y_p, y_s) + tuple(jnp.stack(o) for o in outs)
```

```python
import functools
import math

import numpy as np
import jax
import jax.numpy as jnp
from jax import lax
from jax.experimental import pallas as pl
from jax.experimental.pallas import tpu as pltpu

N_HEADS = 4
HEAD_DIM = 64
V_DIM = 2 * HEAD_DIM
ATT_WIDTH = N_HEADS * V_DIM
QK_WIDTH = N_HEADS * 2 * HEAD_DIM
CONV_K = 3
NUM_BUCKETS = 32
MAX_DISTANCE = 128
PEER_HEADS = 8
PEER_N_KEYS = 128
PEER_TOPK = 16
LN_EPS = 1e-5

LANES = 128
SUBLANES = 8
VMEM_LIMIT_BYTES = 56 * 1024 * 1024
DENSE_LOOKAHEAD = 1

NEG = -1e30
BF16 = jnp.bfloat16
F32 = jnp.float32


def _cparams(sem):
    return pltpu.CompilerParams(dimension_semantics=sem, vmem_limit_bytes=VMEM_LIMIT_BYTES)


def _dot(a, b):
    return jnp.dot(a, b, preferred_element_type=F32)


def _dot_nt(a, b):
    return lax.dot_general(a, b, (((1,), (1,)), ((), ())), preferred_element_type=F32)


def _adaln_kernel(c_ref, w_ref, b_ref, o_ref):
    c = c_ref[...]
    s = c * jax.nn.sigmoid(c)
    o_ref[...] = _dot(s.astype(BF16), w_ref[...].astype(BF16)) + b_ref[...]


def _adaln(c, w_ada, b_ada):
    n, d = c.shape
    width = w_ada.shape[1]
    tn = width // 4
    return pl.pallas_call(
        _adaln_kernel,
        grid=(width // tn,),
        in_specs=[pl.BlockSpec((n, d), lambda j: (0, 0)),
                  pl.BlockSpec((d, tn), lambda j: (0, j)),
                  pl.BlockSpec((1, tn), lambda j: (0, j))],
        out_specs=pl.BlockSpec((n, tn), lambda j: (0, j)),
        out_shape=jax.ShapeDtypeStruct((n, width), F32),
        compiler_params=_cparams(("arbitrary",)),
        name="adaln",
    )(c, w_ada, b_ada.reshape(1, width))


def _inproj_kernel(x_ref, mod_ref, pre_ref, w_ref, cw_ref, q_ref, k_ref, v_ref, oc_ref, cs_ref, zbuf,
                   *, ts, d, cw):
    @pl.when(pl.program_id(1) == 0)
    def _():
        zbuf[SUBLANES - 2:SUBLANES, :] = pre_ref[0]

    shift1 = mod_ref[0, :, 0:d]
    scale1 = mod_ref[0, :, d:2 * d]
    h = x_ref[0] * (1.0 + scale1) + shift1
    proj = _dot(h.astype(BF16), w_ref[...])
    o1 = 2 * QK_WIDTH
    o2 = o1 + ATT_WIDTH
    q_ref[0] = proj[:, :QK_WIDTH]
    k_ref[0] = proj[:, QK_WIDTH:o1]
    v_ref[0] = proj[:, o1:o2]
    gb = proj[:, o2:o2 + cw]
    z = proj[:, o2 + cw:o2 + 2 * cw] * proj[:, o2 + 2 * cw:o2 + 3 * cw]
    zbuf[SUBLANES:SUBLANES + ts, :] = z
    y = (cw_ref[0:1, :] * zbuf[SUBLANES - 2:SUBLANES - 2 + ts, :]
         + cw_ref[1:2, :] * zbuf[SUBLANES - 1:SUBLANES - 1 + ts, :]
         + cw_ref[2:3, :] * z)
    oc_ref[0] = gb * y
    tail = zbuf[SUBLANES + ts - 2:SUBLANES + ts, :]
    cs_ref[0] = tail
    zbuf[SUBLANES - 2:SUBLANES, :] = tail


def _inproj(x, mod, prefix, w_in_bf, conv_w, ts):
    b, s, d = x.shape
    cw = conv_w.shape[1]
    pw = w_in_bf.shape[1]
    blk = lambda width: pl.BlockSpec((1, ts, width), lambda i, j: (i, j, 0))
    outs = pl.pallas_call(
        functools.partial(_inproj_kernel, ts=ts, d=d, cw=cw),
        grid=(b, s // ts),
        in_specs=[blk(d),
                  pl.BlockSpec((1, 1, mod.shape[2]), lambda i, j: (i, 0, 0)),
                  pl.BlockSpec((1, CONV_K - 1, cw), lambda i, j: (i, 0, 0)),
                  pl.BlockSpec((d, pw), lambda i, j: (0, 0)),
                  pl.BlockSpec((CONV_K, cw), lambda i, j: (0, 0))],
        out_specs=[blk(QK_WIDTH), blk(QK_WIDTH), blk(ATT_WIDTH), blk(cw),
                   pl.BlockSpec((1, CONV_K - 1, cw), lambda i, j: (i, 0, 0))],
        out_shape=[jax.ShapeDtypeStruct((b, s, QK_WIDTH), F32),
                   jax.ShapeDtypeStruct((b, s, QK_WIDTH), F32),
                   jax.ShapeDtypeStruct((b, s, ATT_WIDTH), F32),
                   jax.ShapeDtypeStruct((b, s, cw), F32),
                   jax.ShapeDtypeStruct((b, CONV_K - 1, cw), F32)],
        scratch_shapes=[pltpu.VMEM((SUBLANES + ts, cw), F32)],
        compiler_params=_cparams(("arbitrary", "arbitrary")),
        name="inproj",
    )(x, mod, prefix, w_in_bf, conv_w)
    return outs


def _t5_bucket_np(dist):
    dist = np.asarray(dist, np.int64)
    n = np.maximum(dist, 0)
    max_exact = NUM_BUCKETS // 2
    n_large = np.maximum(n, max_exact).astype(np.float64)
    large = max_exact + (np.log(n_large / max_exact) / math.log(MAX_DISTANCE / max_exact)
                         * (NUM_BUCKETS - max_exact)).astype(np.int64)
    large = np.minimum(large, NUM_BUCKETS - 1)
    bucket = np.where(n < max_exact, n, large)
    return np.where(dist < 0, -1, bucket).astype(np.int32)


def _bias_kernel(rb_ref, bk_ref, o_ref, *, rel_to):
    h = pl.program_id(0)
    bk = bk_ref[0]
    base = 0.0 if rel_to is None else rb_ref[rel_to, h]
    acc = jnp.where(bk < 0, NEG, 0.0).astype(F32)
    for b in range(NUM_BUCKETS):
        acc = jnp.where(bk == b, rb_ref[b, h] - base, acc)
    o_ref[0, 0] = acc


def _bias_tiles(rel_bias, buckets, rel_to=None):
    n, r, c = buckets.shape
    return pl.pallas_call(
        functools.partial(_bias_kernel, rel_to=rel_to),
        grid=(N_HEADS, n),
        in_specs=[pl.BlockSpec(memory_space=pltpu.SMEM),
                  pl.BlockSpec((1, r, c), lambda h, i: (i, 0, 0))],
        out_specs=pl.BlockSpec((1, 1, r, c), lambda h, i: (h, i, 0, 0)),
        out_shape=jax.ShapeDtypeStruct((N_HEADS, n, r, c), F32),
        compiler_params=_cparams(("arbitrary", "arbitrary")),
        name="bias_tiles",
    )(rel_bias, jnp.asarray(buckets))


def _far_bucket_from(dist_lo):
    b = _t5_bucket_np(np.arange(dist_lo, dist_lo + 4 * MAX_DISTANCE))
    assert (b == NUM_BUCKETS - 1).all()
    return NUM_BUCKETS - 1


def _diff_lambda(lq1_ref, lk1_ref, lq2_ref, lk2_ref, lam_init):
    a = jnp.sum(lq1_ref[...] * lk1_ref[...], axis=1, keepdims=True)
    b = jnp.sum(lq2_ref[...] * lk2_ref[...], axis=1, keepdims=True)
    return jnp.exp(a) - jnp.exp(b) + lam_init


def _split_q(q):
    lane = lax.broadcasted_iota(jnp.int32, q.shape, 1)
    q1 = jnp.where(lane < HEAD_DIM, q, 0.0)
    q2 = jnp.where(lane >= HEAD_DIM, q, 0.0)
    return jnp.concatenate([q1, q2], axis=0).astype(BF16)


def _online_update(s, v_bf, m_ref, l_ref, acc_ref, rows):
    m_old = m_ref[rows]
    m_new = jnp.maximum(m_old, jnp.max(s, axis=1, keepdims=True))
    p = jnp.exp(s - m_new)
    alpha = jnp.exp(m_old - m_new)
    l_ref[rows] = alpha * l_ref[rows] + jnp.sum(p, axis=1, keepdims=True)
    acc_ref[rows] = alpha * acc_ref[rows] + _dot(p.astype(BF16), v_bf)
    m_ref[rows] = m_new


def _diff_finish(acc, l, lam, sw, nq, lam_init):
    o = acc[:nq] / l[:nq] - lam * (acc[nq:] / l[nq:])
    o = o * lax.rsqrt(jnp.mean(o * o, axis=1, keepdims=True) + LN_EPS)
    return o * sw * (1.0 - lam_init)


def _pattn_kernel(q_ref, k_ref, v_ref, bt_ref, lq1, lk1, lq2, lk2, sw_ref, o_ref,
                  kb_ref, vt_ref, m_ref, l_ref, acc_ref, *, qb, nh, lam_init):
    i = pl.program_id(2)
    n_kt = kb_ref.shape[1]

    @pl.when(i == 0)
    def _():
        for g in range(nh):
            cols = slice(g * V_DIM, (g + 1) * V_DIM)
            for c in range(n_kt):
                kb_ref[g, c] = k_ref[0, c * qb:(c + 1) * qb, cols].astype(BF16)
                vt_ref[g, c] = v_ref[0, c * qb:(c + 1) * qb, cols].T.astype(BF16)

    q2 = [_split_q(q_ref[0, :, g * V_DIM:(g + 1) * V_DIM] * (HEAD_DIM ** -0.5)) for g in range(nh)]
    m_ref[...] = jnp.full(m_ref.shape, NEG, F32)
    l_ref[...] = jnp.zeros(l_ref.shape, F32)
    acc_ref[...] = jnp.zeros(acc_ref.shape, F32)

    def tiles(items):
        old = [(m_ref[g], l_ref[g], acc_ref[g]) for g in range(nh)]
        new = []
        for g in range(nh):
            sts = []
            for j, slot in items:
                st = _dot_nt(kb_ref[g, j], q2[g])
                sts.append(st if slot is None else st + bt_ref[g, slot])
            m_old, l_old, acc_old = old[g]
            m_new = m_old
            for st in sts:
                m_new = jnp.maximum(m_new, jnp.max(st, axis=0, keepdims=True))
            alpha = jnp.exp(m_old - m_new)
            l_new = alpha * l_old
            acc_new = alpha * acc_old
            for (j, _), st in zip(items, sts):
                p = jnp.exp(st - m_new)
                l_new = l_new + jnp.sum(p, axis=0, keepdims=True)
                acc_new = acc_new + _dot(vt_ref[g, j], p.astype(BF16))
            new.append((m_new, l_new, acc_new))
        for g in range(nh):
            m_ref[g], l_ref[g], acc_ref[g] = new[g]

    n_far = jnp.maximum(i - 1, 0)

    n_quad = n_far // 4

    def far_quad(jj, carry):
        tiles([(4 * jj + r, None) for r in range(4)])
        return carry

    lax.fori_loop(0, n_quad, far_quad, 0)

    @pl.when(n_far % 4 >= 2)
    def _():
        tiles([(4 * n_quad, None), (4 * n_quad + 1, None)])

    @pl.when(n_far % 2 == 1)
    def _():
        tiles([(n_far - 1, None)])

    @pl.when(i >= 1)
    def _():
        tiles([(i - 1, 1), (i, 0)])

    @pl.when(i == 0)
    def _():
        tiles([(i, 0)])

    lam = _diff_lambda(lq1, lk1, lq2, lk2, lam_init)
    for g in range(nh):
        acc = acc_ref[g]
        l = l_ref[g]
        ot = acc[:, :qb] / l[:, :qb] - lam * (acc[:, qb:] / l[:, qb:])
        ot = ot * lax.rsqrt(jnp.mean(ot * ot, axis=0, keepdims=True) + LN_EPS)
        o_ref[0, :, g * V_DIM:(g + 1) * V_DIM] = ot.T * sw_ref[...] * (1.0 - lam_init)


def _prompt_attention(q, k, v, rel_bias, lq1, lk1, lq2, lk2, subln_w, lam_init, qb, nh):
    b, s, _ = q.shape
    kk = np.arange(qb)[:, None]
    qq = np.tile(np.arange(qb), 2)[None, :]
    buckets = np.stack([_t5_bucket_np(qq - kk), _t5_bucket_np(qb + qq - kk)])
    far_bucket = _far_bucket_from(qb + 1)
    bt = _bias_tiles(rel_bias, buckets, rel_to=far_bucket)
    vec = lambda n: pl.BlockSpec((1, n), lambda bi, h, i: (0, 0))
    gw = nh * V_DIM
    n_kt = s // qb
    return pl.pallas_call(
        functools.partial(_pattn_kernel, qb=qb, nh=nh, lam_init=lam_init),
        grid=(b, N_HEADS // nh, n_kt),
        in_specs=[pl.BlockSpec((1, qb, gw), lambda bi, h, i: (bi, i, h)),
                  pl.BlockSpec((1, s, gw), lambda bi, h, i: (bi, 0, h)),
                  pl.BlockSpec((1, s, gw), lambda bi, h, i: (bi, 0, h)),
                  pl.BlockSpec((nh, 2, qb, 2 * qb), lambda bi, h, i: (h, 0, 0, 0)),
                  vec(HEAD_DIM), vec(HEAD_DIM), vec(HEAD_DIM), vec(HEAD_DIM), vec(V_DIM)],
        out_specs=pl.BlockSpec((1, qb, gw), lambda bi, h, i: (bi, i, h)),
        out_shape=jax.ShapeDtypeStruct((b, s, ATT_WIDTH), F32),
        scratch_shapes=[pltpu.VMEM((nh, n_kt, qb, V_DIM), BF16), pltpu.VMEM((nh, n_kt, V_DIM, qb), BF16),
                        pltpu.VMEM((nh, 1, 2 * qb), F32), pltpu.VMEM((nh, 1, 2 * qb), F32),
                        pltpu.VMEM((nh, V_DIM, 2 * qb), F32)],
        compiler_params=_cparams(("arbitrary", "arbitrary", "arbitrary")),
        name="prompt_attention",
    )(q, k, v, bt, lq1, lk1, lq2, lk2, subln_w)


def _sattn_kernel(pt_ref, q_ref, kn_ref, vn_ref, bt_ref, bn_ref, lq1, lk1, lq2, lk2, sw_ref, *rest,
                  pps, t, lam_init):
    kp = rest[:pps]
    vp = rest[pps:2 * pps]
    o_ref = rest[2 * pps]
    m_ref, l_ref, acc_ref = rest[2 * pps + 1:]
    j = pl.program_id(1)
    nj = pl.num_programs(1)
    nq = 2 * t

    @pl.when(j == 0)
    def _():
        m_ref[...] = jnp.full(m_ref.shape, NEG, F32)
        l_ref[...] = jnp.zeros(l_ref.shape, F32)
        acc_ref[...] = jnp.zeros(acc_ref.shape, F32)

    q = q_ref[0] * (HEAD_DIM ** -0.5)
    m_all, l_all, acc_all = m_ref[...], l_ref[...], acc_ref[...]
    new_m, new_l, new_acc = [], [], []
    for h in range(N_HEADS):
        cols = slice(h * V_DIM, (h + 1) * V_DIM)
        rows = slice(h * nq, (h + 1) * nq)
        q2 = _split_q(q[:, cols])
        s = jnp.concatenate([_dot(q2, kp[r][0, h].reshape(V_DIM, -1).astype(BF16)) for r in range(pps)],
                            axis=1)
        s = s + bt_ref[h, 0]
        m_old = m_all[rows]
        m_new = jnp.maximum(m_old, jnp.max(s, axis=1, keepdims=True))
        p = jnp.exp(s - m_new)
        alpha = jnp.exp(m_old - m_new)
        new_l.append(alpha * l_all[rows] + jnp.sum(p, axis=1, keepdims=True))
        p = p.astype(BF16)
        vrow = pl.ds(h, LANES, stride=N_HEADS)
        pv = _dot(p[:, 0:LANES], vp[0][vrow, :].astype(BF16))
        for r in range(1, pps):
            pv = pv + _dot(p[:, r * LANES:(r + 1) * LANES], vp[r][vrow, :].astype(BF16))
        new_acc.append(alpha * acc_all[rows] + pv)
        new_m.append(m_new)
    m_ref[...] = jnp.concatenate(new_m, axis=0)
    l_ref[...] = jnp.concatenate(new_l, axis=0)
    acc_ref[...] = jnp.concatenate(new_acc, axis=0)

    @pl.when(j == nj - 1)
    def _():
        lam = _diff_lambda(lq1, lk1, lq2, lk2, lam_init)
        for h in range(N_HEADS):
            cols = slice(h * V_DIM, (h + 1) * V_DIM)
            rows = slice(h * nq, (h + 1) * nq)
            q2 = _split_q(q[:, cols])
            s = _dot_nt(q2, kn_ref[0, :, cols].astype(BF16)) + bn_ref[h, 0]
            _online_update(s, vn_ref[0, :, cols].astype(BF16), m_ref, l_ref, acc_ref, rows)
            o_ref[0, :, cols] = _diff_finish(acc_ref[rows], l_ref[rows], lam, sw_ref[...], t, lam_init)


def _sample_attention(q, k_new, v_new, cache_k, cache_v, page_table, rel_bias,
                      lq1, lk1, lq2, lk2, subln_w, lam_init, pps):
    bs, t, _ = q.shape
    ck, cv = cache_k, cache_v
    page = ck.shape[4]
    assert page == LANES
    n_pages = page_table.shape[1]
    past = n_pages * page
    nj = n_pages // pps
    chunk = pps * page
    tq = np.tile(np.arange(t), 2)[:, None]
    kc = np.arange(chunk)[None, :]
    last = _t5_bucket_np(past + tq - (past - chunk + kc))
    assert (_t5_bucket_np(past - (past - chunk) + 1 + np.arange(4 * MAX_DISTANCE)) == NUM_BUCKETS - 1).all()
    far = np.full_like(last, NUM_BUCKETS - 1)
    kn = np.arange(page)[None, :]
    newb = np.where(kn < t, _t5_bucket_np(tq - kn), -1).astype(np.int32)
    bt = _bias_tiles(rel_bias, np.stack([far, last]))
    bn = _bias_tiles(rel_bias, newb[None])
    knp = jnp.pad(k_new, ((0, 0), (0, page - t), (0, 0)))
    vnp = jnp.pad(v_new, ((0, 0), (0, page - t), (0, 0)))
    vec = lambda n: pl.BlockSpec((1, n), lambda b, j, pt: (0, 0))

    def kpage_spec(r):
        return pl.BlockSpec((1,) + ck.shape[1:], lambda b, j, pt: (pt[b, j * pps + r], 0, 0, 0, 0))

    def vpage_spec(r):
        return pl.BlockSpec((page * N_HEADS, V_DIM), lambda b, j, pt: (pt[b, j * pps + r], 0))

    grid_spec = pltpu.PrefetchScalarGridSpec(
        num_scalar_prefetch=1,
        grid=(bs, nj),
        in_specs=[pl.BlockSpec((1, t, QK_WIDTH), lambda b, j, pt: (b, 0, 0)),
                  pl.BlockSpec((1, page, QK_WIDTH), lambda b, j, pt: (b, 0, 0)),
                  pl.BlockSpec((1, page, ATT_WIDTH), lambda b, j, pt: (b, 0, 0)),
                  pl.BlockSpec((N_HEADS, 1, 2 * t, chunk), lambda b, j, pt: (0, (j == nj - 1).astype(jnp.int32), 0, 0)),
                  pl.BlockSpec((N_HEADS, 1, 2 * t, page), lambda b, j, pt: (0, 0, 0, 0)),
                  vec(HEAD_DIM), vec(HEAD_DIM), vec(HEAD_DIM), vec(HEAD_DIM), vec(V_DIM)]
                 + [kpage_spec(r) for r in range(pps)] + [vpage_spec(r) for r in range(pps)],
        out_specs=pl.BlockSpec((1, t, ATT_WIDTH), lambda b, j, pt: (b, 0, 0)),
        scratch_shapes=[pltpu.VMEM((N_HEADS * 2 * t, 1), F32), pltpu.VMEM((N_HEADS * 2 * t, 1), F32),
                        pltpu.VMEM((N_HEADS * 2 * t, V_DIM), F32)],
    )
    return pl.pallas_call(
        functools.partial(_sattn_kernel, pps=pps, t=t, lam_init=lam_init),
        grid_spec=grid_spec,
        out_shape=jax.ShapeDtypeStruct((bs, t, ATT_WIDTH), F32),
        compiler_params=_cparams(("arbitrary", "arbitrary")),
        name="sample_attention",
    )(page_table, q, knp, vnp, bt, bn, lq1, lk1, lq2, lk2, subln_w, *([ck] * pps), *([cv] * pps))


def _layer_norm(y, g, b):
    mu = jnp.mean(y, axis=1, keepdims=True)
    yc = y - mu
    var = jnp.mean(yc * yc, axis=1, keepdims=True)
    return yc * lax.rsqrt(var + LN_EPS) * g + b


def _outproj_kernel(oa_ref, oc_ref, x_ref, mod_ref, w_ref, g_ref, b_ref, x1_ref, h2_ref, *, d, alpha):
    aw = oa_ref.shape[2]
    mix = _dot(oa_ref[0].astype(BF16), w_ref[0:aw, :]) + _dot(oc_ref[0].astype(BF16), w_ref[aw:, :])
    gate1 = mod_ref[0, :, 2 * d:3 * d]
    shift2 = mod_ref[0, :, 3 * d:4 * d]
    scale2 = mod_ref[0, :, 4 * d:5 * d]
    x1 = _layer_norm(alpha * x_ref[0] + gate1 * mix, g_ref[...], b_ref[...])
    x1_ref[0] = x1
    h2_ref[0] = x1 * (1.0 + scale2) + shift2


def _outproj(oa, oc, x, mod, w_out_bf, g, bb, alpha, ts):
    b, s, d = x.shape
    blk = lambda width: pl.BlockSpec((1, ts, width), lambda i, j: (i, j, 0))
    vec = pl.BlockSpec((1, d), lambda i, j: (0, 0))
    return pl.pallas_call(
        functools.partial(_outproj_kernel, d=d, alpha=alpha),
        grid=(b, s // ts),
        in_specs=[blk(oa.shape[2]), blk(oc.shape[2]), blk(d),
                  pl.BlockSpec((1, 1, mod.shape[2]), lambda i, j: (i, 0, 0)),
                  pl.BlockSpec(w_out_bf.shape, lambda i, j: (0, 0)), vec, vec],
        out_specs=[blk(d), blk(d)],
        out_shape=[jax.ShapeDtypeStruct((b, s, d), F32), jax.ShapeDtypeStruct((b, s, d), F32)],
        compiler_params=_cparams(("arbitrary", "arbitrary")),
        name="outproj_ln1",
    )(oa, oc, x, mod, w_out_bf, g, bb)


_CAND_BLOCKS = [(i, 16 if i == 0 else 8, PEER_TOPK // (i + 1)) for i in range(8)]


_STACK_DEPTH = 4
_STACK_SORT = ((0, 1), (2, 3), (1, 2), (0, 1), (2, 3), (1, 2))


def _top16_rows(x, n, v_ref, i_ref):
    sub = lax.broadcasted_iota(jnp.int32, (SUBLANES, x.shape[1]), 0)
    n_grp = n // (SUBLANES * _STACK_DEPTH)
    val, key = [], []
    for g in range(n_grp):
        rows = [(g * _STACK_DEPTH + l) * SUBLANES for l in range(_STACK_DEPTH)]
        sv = [x[r0:r0 + SUBLANES] for r0 in rows]
        sk = [sub + r0 for r0 in rows]
        for a, b in _STACK_SORT:
            swap = sv[b] > sv[a]
            sv[a], sv[b] = jnp.where(swap, sv[b], sv[a]), jnp.where(swap, sv[a], sv[b])
            sk[a], sk[b] = jnp.where(swap, sk[b], sk[a]), jnp.where(swap, sk[a], sk[b])
        val.append(sv)
        key.append(sk)
    for r in range(PEER_TOPK):
        top = val[0][0]
        for g in range(1, n_grp):
            top = jnp.maximum(top, val[g][0])
        m = jnp.max(top, axis=0, keepdims=True)
        cand = jnp.where(val[0][0] == m, key[0][0], n)
        for g in range(1, n_grp):
            cand = jnp.minimum(cand, jnp.where(val[g][0] == m, key[g][0], n))
        idx = jnp.min(cand, axis=0, keepdims=True)
        v_ref[r:r + 1, :] = m
        i_ref[r:r + 1, :] = idx
        if r + 1 < PEER_TOPK:
            for g in range(n_grp):
                hit = key[g][0] == idx
                for l in range(_STACK_DEPTH - 1):
                    val[g][l] = jnp.where(hit, val[g][l + 1], val[g][l])
                    key[g][l] = jnp.where(hit, key[g][l + 1], key[g][l])
                val[g][-1] = jnp.where(hit, -jnp.inf, val[g][-1])


def _route_kernel(h_ref, w_ref, sk_ref, e_ref, g_ref, qt_ref, sv_ref, si_ref, cv_ref, ce_ref, et_ref, gt_ref,
                  *, tb):
    nlb = tb // LANES
    hb = h_ref[...].astype(BF16)
    qt_ref[...] = _dot_nt(w_ref[...], hb)
    half = sk_ref.shape[2]

    def stage1(hp, carry):
        row0 = pl.multiple_of(hp * half, half)
        st = _dot(sk_ref[hp].astype(BF16), qt_ref[pl.ds(row0, half), :].astype(BF16))
        for lb in range(nlb):
            _top16_rows(st[:, lb * LANES:(lb + 1) * LANES], PEER_N_KEYS,
                        sv_ref.at[hp, :, lb * LANES:(lb + 1) * LANES],
                        si_ref.at[hp, :, lb * LANES:(lb + 1) * LANES])
        return carry

    lax.fori_loop(0, 2 * PEER_HEADS, stage1, 0)

    sub8 = lax.broadcasted_iota(jnp.int32, (8, LANES), 0)
    sub16 = lax.broadcasted_iota(jnp.int32, (16, LANES), 0)

    def stage2(h, carry):
        for lb in range(nlb):
            ls = slice(lb * LANES, (lb + 1) * LANES)
            sv0 = sv_ref[2 * h, :, ls]
            sv1 = sv_ref[2 * h + 1, :, ls]
            si0 = si_ref[2 * h, :, ls]
            si1 = si_ref[2 * h + 1, :, ls]
            vals, flats = [], []
            for i, rows, cnt in _CAND_BLOCKS:
                sub = sub16 if rows == 16 else sub8
                vals.append(jnp.where(sub < cnt, sv0[i:i + 1] + sv1[0:rows], -jnp.inf))
                flats.append(i * PEER_TOPK + sub)
            vals.append(sv0[8:16] + sv1[0:1])
            flats.append((sub8 + 8) * PEER_TOPK)
            cand = jnp.concatenate(vals, axis=0)
            flat = jnp.concatenate(flats, axis=0)
            big = PEER_TOPK * PEER_TOPK
            for r in range(PEER_TOPK):
                m = jnp.max(cand, axis=0, keepdims=True)
                fl = jnp.min(jnp.where(cand == m, flat, big), axis=0, keepdims=True)
                cv_ref[r:r + 1, :] = m
                ce_ref[r:r + 1, :] = fl
                cand = jnp.where(flat == fl, -jnp.inf, cand)
            cv = cv_ref[...]
            ex = jnp.exp(cv - cv[0:1])
            g = ex / jnp.sum(ex, axis=0, keepdims=True)
            fl = ce_ref[...]
            fi = fl // PEER_TOPK
            fj = fl - fi * PEER_TOPK
            ea = jnp.zeros_like(fl)
            eb = jnp.zeros_like(fl)
            for i in range(PEER_TOPK):
                ea = jnp.where(fi == i, si0[i:i + 1], ea)
                eb = jnp.where(fj == i, si1[i:i + 1], eb)
            row0 = pl.multiple_of(h * PEER_TOPK, PEER_TOPK)
            gt_ref[pl.ds(row0, PEER_TOPK), ls] = g
            et_ref[pl.ds(row0, PEER_TOPK), ls] = ea * PEER_N_KEYS + eb
        return carry

    lax.fori_loop(0, PEER_HEADS, stage2, 0)
    e_ref[...] = et_ref[...].T
    g_ref[...] = gt_ref[...].T


def _peer_route(h2, w_pq_t_bf, sub_keys, tb):
    t, d = h2.shape
    qw = w_pq_t_bf.shape[0]
    hp, n_keys, half = sub_keys.shape
    slots = PEER_HEADS * PEER_TOPK
    return pl.pallas_call(
        functools.partial(_route_kernel, tb=tb),
        grid=(t // tb,),
        in_specs=[pl.BlockSpec((tb, d), lambda i: (i, 0)),
                  pl.BlockSpec((qw, d), lambda i: (0, 0)),
                  pl.BlockSpec((hp, n_keys, half), lambda i: (0, 0, 0))],
        out_specs=[pl.BlockSpec((tb, slots), lambda i: (i, 0)), pl.BlockSpec((tb, slots), lambda i: (i, 0))],
        out_shape=[jax.ShapeDtypeStruct((t, slots), jnp.int32), jax.ShapeDtypeStruct((t, slots), F32)],
        scratch_shapes=[pltpu.VMEM((qw, tb), F32),
                        pltpu.VMEM((hp, PEER_TOPK, tb), F32), pltpu.VMEM((hp, PEER_TOPK, tb), jnp.int32),
                        pltpu.VMEM((PEER_TOPK, LANES), F32), pltpu.VMEM((PEER_TOPK, LANES), jnp.int32),
                        pltpu.VMEM((slots, tb), jnp.int32), pltpu.VMEM((slots, tb), F32)],
        compiler_params=_cparams(("arbitrary",)),
        name="peer_route",
    )(h2, w_pq_t_bf, sub_keys)


def _expand_kernel(e_ref, g_ref, w_ref, *, tb):
    iota = lax.broadcasted_iota(jnp.int32, (PEER_N_KEYS, e_ref.shape[1]), 0)

    def body(t, carry):
        er = e_ref[pl.ds(t, 1), :]
        gr = g_ref[pl.ds(t, 1), :]
        ea = er // PEER_N_KEYS
        eb = er - ea * PEER_N_KEYS
        oa = jnp.where(iota == ea, gr, 0.0).astype(BF16)
        ob = jnp.where(iota == eb, 1.0, 0.0).astype(BF16)
        w_ref[t] = _dot_nt(oa, ob)
        return carry

    lax.fori_loop(0, tb, body, 0, unroll=32)


def _peer_expand(e, g, tb):
    t, slots = e.shape
    return pl.pallas_call(
        functools.partial(_expand_kernel, tb=tb),
        grid=(t // tb,),
        in_specs=[pl.BlockSpec((tb, slots), lambda i: (i, 0)), pl.BlockSpec((tb, slots), lambda i: (i, 0))],
        out_specs=pl.BlockSpec((tb, PEER_N_KEYS, PEER_N_KEYS), lambda i: (i, 0, 0)),
        out_shape=jax.ShapeDtypeStruct((t, PEER_N_KEYS, PEER_N_KEYS), F32),
        compiler_params=_cparams(("arbitrary",)),
        name="peer_expand",
    )(e, g)


def _dense_kernel(h_ref, ut_ref, v_ref, w_ref, f_ref, hb_ref, *, eb, ec, look):
    j = pl.program_id(1)

    @pl.when(j == 0)
    def _():
        hb_ref[...] = h_ref[...].astype(BF16)
        f_ref[...] = jnp.zeros(f_ref.shape, F32)

    hb = hb_ref[...]
    nc = eb // ec
    pre = lambda c: _dot(hb, ut_ref[:, c * ec:(c + 1) * ec])
    queue = [pre(c) for c in range(min(look, nc))]
    for c in range(nc):
        a = queue.pop(0)
        if c + look < nc:
            queue.append(pre(c + look))
        a = 0.5 * a * (1.0 + lax.erf(a * (2.0 ** -0.5)))
        parts = []
        for al in range(ec // PEER_N_KEYS):
            ai = c * (ec // PEER_N_KEYS) + al
            parts.append((a[:, al * PEER_N_KEYS:(al + 1) * PEER_N_KEYS] * w_ref[:, ai, :]).astype(BF16))
        wa = jnp.concatenate(parts, axis=1)
        f_ref[...] += _dot(wa, v_ref[c * ec:(c + 1) * ec, :])


def _peer_dense(h2, u_t_bf, v_bf, w, tb, eb, ec):
    t, d = h2.shape
    n_exp = v_bf.shape[0]
    return pl.pallas_call(
        functools.partial(_dense_kernel, eb=eb, ec=ec, look=DENSE_LOOKAHEAD),
        grid=(t // tb, n_exp // eb),
        in_specs=[pl.BlockSpec((tb, d), lambda i, j: (i, 0)),
                  pl.BlockSpec((d, eb), lambda i, j: (0, j)),
                  pl.BlockSpec((eb, d), lambda i, j: (j, 0)),
                  pl.BlockSpec((tb, eb // PEER_N_KEYS, PEER_N_KEYS), lambda i, j: (i, j, 0))],
        out_specs=pl.BlockSpec((tb, d), lambda i, j: (i, 0)),
        out_shape=jax.ShapeDtypeStruct((t, d), F32),
        scratch_shapes=[pltpu.VMEM((tb, d), BF16)],
        compiler_params=_cparams(("arbitrary", "arbitrary")),
        name="peer_dense",
    )(h2, u_t_bf, v_bf, w)


def _final_kernel(x1_ref, f_ref, mod_ref, g_ref, b_ref, y_ref, *, d, alpha):
    gate2 = mod_ref[0, :, 5 * d:6 * d]
    y_ref[0] = _layer_norm(alpha * x1_ref[0] + gate2 * f_ref[0], g_ref[...], b_ref[...])


def _final_ln(x1, f, mod, g, bb, alpha, ts):
    b, s, d = x1.shape
    blk = pl.BlockSpec((1, ts, d), lambda i, j: (i, j, 0))
    vec = pl.BlockSpec((1, d), lambda i, j: (0, 0))
    return pl.pallas_call(
        functools.partial(_final_kernel, d=d, alpha=alpha),
        grid=(b, s // ts),
        in_specs=[blk, blk, pl.BlockSpec((1, 1, mod.shape[2]), lambda i, j: (i, 0, 0)), vec, vec],
        out_specs=blk,
        out_shape=jax.ShapeDtypeStruct((b, s, d), F32),
        compiler_params=_cparams(("arbitrary", "arbitrary")),
        name="final_ln2",
    )(x1, f, mod, g, bb)


def _pick(n, prefs):
    for p in prefs:
        if n % p == 0:
            return p
    return n


def _decoder_layer(x, mod, prefix, attend, p, alpha):
    b, s, d = x.shape
    ts = _pick(s, (512, 256, 128))
    q, k, v, oc, conv_state = _inproj(x, mod, prefix, p["w_in"], p["conv_w"], ts)
    oa = attend(q, k, v)
    x1, h2 = _outproj(oa, oc, x, mod, p["w_out"], p["ln1_g"], p["ln1_b"], alpha, ts)
    t = b * s
    h2f = h2.reshape(t, d)
    e, g = _peer_route(h2f, p["w_pq_t"], p["sub_keys"], _pick(t, (1024, 512, 256, 128)))
    w = _peer_expand(e, g, _pick(t, (128,)))
    n_exp = p["peer_v"].shape[0]
    f = _peer_dense(h2f, p["peer_u_t"], p["peer_v"], w, _pick(t, (512, 256, 128)),
                    _pick(n_exp, (2048,)), 1024)
    y = _final_ln(x1, f.reshape(b, s, d), mod, p["ln2_g"], p["ln2_b"], alpha, ts)
    return y, k, v, conv_state


def kernel(x_prompt, x_sample, cache_k, cache_v, state_conv, page_table, c_prompt, c_sample, rel_bias,
           w_ada, b_ada, w_in, lambda_q1, lambda_k1, lambda_q2, lambda_k2, subln_w, conv_w, w_out,
           ln1_g, ln1_b, w_pq, sub_keys, peer_u, peer_v, ln2_g, ln2_b):
    depth = w_ada.shape[0]
    bp, sp, d = x_prompt.shape
    bs, tsmp, _ = x_sample.shape
    alpha = (2.0 * depth) ** 0.25
    y_p, y_s = x_prompt, x_sample
    outs = [[] for _ in range(6)]
    zero_prefix = jnp.zeros((bp, CONV_K - 1, conv_w.shape[2]), x_prompt.dtype)
    c_all = jnp.concatenate([c_prompt, c_sample], axis=0)
    n_pool, page = cache_k.shape[1], cache_k.shape[2]
    ck_all = jnp.transpose(cache_k, (0, 1, 3, 4, 5, 2)).reshape(depth * n_pool, N_HEADS, 2, HEAD_DIM, page)
    cv_all = cache_v.reshape(depth * n_pool * page * N_HEADS, V_DIM)
    row = lambda a: a.reshape(1, -1)
    for layer in range(depth):
        lam_init = 0.8 - 0.6 * math.exp(-0.3 * layer)
        p = dict(w_in=w_in[layer].astype(BF16), conv_w=conv_w[layer], w_out=w_out[layer].astype(BF16),
                 ln1_g=row(ln1_g[layer]), ln1_b=row(ln1_b[layer]),
                 w_pq_t=w_pq[layer].T.astype(BF16),
                 sub_keys=sub_keys[layer].reshape(2 * PEER_HEADS, PEER_N_KEYS, -1),
                 peer_u_t=peer_u[layer].T.astype(BF16), peer_v=peer_v[layer].astype(BF16),
                 ln2_g=row(ln2_g[layer]), ln2_b=row(ln2_b[layer]))
        lam_args = (row(lambda_q1[layer]), row(lambda_k1[layer]), row(lambda_q2[layer]),
                    row(lambda_k2[layer]), row(subln_w[layer]))
        mod = _adaln(c_all, w_ada[layer], b_ada[layer])[:, None, :]
        attend_p = lambda q, k, v: _prompt_attention(q, k, v, rel_bias, *lam_args, lam_init, 256, 2)
        pt_layer = page_table + layer * n_pool
        attend_s = lambda q, k, v: _sample_attention(q, k, v, ck_all, cv_all, pt_layer,
                                                     rel_bias, *lam_args, lam_init,
                                                     _pick(page_table.shape[1], (16, 8, 4, 2)))
        y_p, kp, vp, cp = _decoder_layer(y_p, mod[:bp], zero_prefix, attend_p, p, alpha)
        y_s, ks, vs, cs = _decoder_layer(y_s, mod[bp:], state_conv[layer], attend_s, p, alpha)
        for lst, val in zip(outs, (kp.reshape(bp, sp, N_HEADS, 2, HEAD_DIM), vp.reshape(bp, sp, N_HEADS, V_DIM), cp,
                                   ks.reshape(bs, tsmp, N_HEADS, 2, HEAD_DIM), vs.reshape(bs, tsmp, N_HEADS, V_DIM), cs)):
            lst.append(val)
    return (y_p, y_s) + tuple(jnp.stack(o) for o in outs)
```

```python
import functools
import math

import numpy as np
import jax
import jax.numpy as jnp
from jax import lax
from jax.experimental import pallas as pl
from jax.experimental.pallas import tpu as pltpu

N_HEADS = 4
HEAD_DIM = 64
V_DIM = 2 * HEAD_DIM
ATT_WIDTH = N_HEADS * V_DIM
QK_WIDTH = N_HEADS * 2 * HEAD_DIM
CONV_K = 3
NUM_BUCKETS = 32
MAX_DISTANCE = 128
PEER_HEADS = 8
PEER_N_KEYS = 128
PEER_TOPK = 16
LN_EPS = 1e-5

LANES = 128
SUBLANES = 8
VMEM_LIMIT_BYTES = 56 * 1024 * 1024
DENSE_LOOKAHEAD = 1
DENSE_KSLICE = 256

NEG = -1e30
BF16 = jnp.bfloat16
F32 = jnp.float32


def _cparams(sem):
    return pltpu.CompilerParams(dimension_semantics=sem, vmem_limit_bytes=VMEM_LIMIT_BYTES)


def _dot(a, b):
    return jnp.dot(a, b, preferred_element_type=F32)


def _dot_nt(a, b):
    return lax.dot_general(a, b, (((1,), (1,)), ((), ())), preferred_element_type=F32)


def _adaln_kernel(c_ref, w_ref, b_ref, o_ref):
    c = c_ref[...]
    s = c * jax.nn.sigmoid(c)
    o_ref[...] = _dot(s.astype(BF16), w_ref[...].astype(BF16)) + b_ref[...]


def _adaln(c, w_ada, b_ada):
    n, d = c.shape
    width = w_ada.shape[1]
    tn = width // 4
    return pl.pallas_call(
        _adaln_kernel,
        grid=(width // tn,),
        in_specs=[pl.BlockSpec((n, d), lambda j: (0, 0)),
                  pl.BlockSpec((d, tn), lambda j: (0, j)),
                  pl.BlockSpec((1, tn), lambda j: (0, j))],
        out_specs=pl.BlockSpec((n, tn), lambda j: (0, j)),
        out_shape=jax.ShapeDtypeStruct((n, width), F32),
        compiler_params=_cparams(("arbitrary",)),
        name="adaln",
    )(c, w_ada, b_ada.reshape(1, width))


def _inproj_kernel(x_ref, mod_ref, pre_ref, w_ref, cw_ref, q_ref, k_ref, v_ref, oc_ref, cs_ref, zbuf,
                   *, ts, d, cw):
    @pl.when(pl.program_id(1) == 0)
    def _():
        zbuf[SUBLANES - 2:SUBLANES, :] = pre_ref[0]

    shift1 = mod_ref[0, :, 0:d]
    scale1 = mod_ref[0, :, d:2 * d]
    h = x_ref[0] * (1.0 + scale1) + shift1
    proj = _dot(h.astype(BF16), w_ref[...])
    o1 = 2 * QK_WIDTH
    o2 = o1 + ATT_WIDTH
    q_ref[0] = proj[:, :QK_WIDTH]
    k_ref[0] = proj[:, QK_WIDTH:o1]
    v_ref[0] = proj[:, o1:o2]
    gb = proj[:, o2:o2 + cw]
    z = proj[:, o2 + cw:o2 + 2 * cw] * proj[:, o2 + 2 * cw:o2 + 3 * cw]
    zbuf[SUBLANES:SUBLANES + ts, :] = z
    y = (cw_ref[0:1, :] * zbuf[SUBLANES - 2:SUBLANES - 2 + ts, :]
         + cw_ref[1:2, :] * zbuf[SUBLANES - 1:SUBLANES - 1 + ts, :]
         + cw_ref[2:3, :] * z)
    oc_ref[0] = gb * y
    tail = zbuf[SUBLANES + ts - 2:SUBLANES + ts, :]
    cs_ref[0] = tail
    zbuf[SUBLANES - 2:SUBLANES, :] = tail


def _inproj(x, mod, prefix, w_in_bf, conv_w, ts):
    b, s, d = x.shape
    cw = conv_w.shape[1]
    pw = w_in_bf.shape[1]
    blk = lambda width: pl.BlockSpec((1, ts, width), lambda i, j: (i, j, 0))
    outs = pl.pallas_call(
        functools.partial(_inproj_kernel, ts=ts, d=d, cw=cw),
        grid=(b, s // ts),
        in_specs=[blk(d),
                  pl.BlockSpec((1, 1, mod.shape[2]), lambda i, j: (i, 0, 0)),
                  pl.BlockSpec((1, CONV_K - 1, cw), lambda i, j: (i, 0, 0)),
                  pl.BlockSpec((d, pw), lambda i, j: (0, 0)),
                  pl.BlockSpec((CONV_K, cw), lambda i, j: (0, 0))],
        out_specs=[blk(QK_WIDTH), blk(QK_WIDTH), blk(ATT_WIDTH), blk(cw),
                   pl.BlockSpec((1, CONV_K - 1, cw), lambda i, j: (i, 0, 0))],
        out_shape=[jax.ShapeDtypeStruct((b, s, QK_WIDTH), F32),
                   jax.ShapeDtypeStruct((b, s, QK_WIDTH), F32),
                   jax.ShapeDtypeStruct((b, s, ATT_WIDTH), F32),
                   jax.ShapeDtypeStruct((b, s, cw), F32),
                   jax.ShapeDtypeStruct((b, CONV_K - 1, cw), F32)],
        scratch_shapes=[pltpu.VMEM((SUBLANES + ts, cw), F32)],
        compiler_params=_cparams(("arbitrary", "arbitrary")),
        name="inproj",
    )(x, mod, prefix, w_in_bf, conv_w)
    return outs


def _t5_bucket_np(dist):
    dist = np.asarray(dist, np.int64)
    n = np.maximum(dist, 0)
    max_exact = NUM_BUCKETS // 2
    n_large = np.maximum(n, max_exact).astype(np.float64)
    large = max_exact + (np.log(n_large / max_exact) / math.log(MAX_DISTANCE / max_exact)
                         * (NUM_BUCKETS - max_exact)).astype(np.int64)
    large = np.minimum(large, NUM_BUCKETS - 1)
    bucket = np.where(n < max_exact, n, large)
    return np.where(dist < 0, -1, bucket).astype(np.int32)


def _bias_kernel(rb_ref, bk_ref, o_ref, *, rel_to):
    h = pl.program_id(0)
    bk = bk_ref[0]
    base = 0.0 if rel_to is None else rb_ref[rel_to, h]
    acc = jnp.where(bk < 0, NEG, 0.0).astype(F32)
    for b in range(NUM_BUCKETS):
        acc = jnp.where(bk == b, rb_ref[b, h] - base, acc)
    o_ref[0, 0] = acc


def _bias_tiles(rel_bias, buckets, rel_to=None):
    n, r, c = buckets.shape
    return pl.pallas_call(
        functools.partial(_bias_kernel, rel_to=rel_to),
        grid=(N_HEADS, n),
        in_specs=[pl.BlockSpec(memory_space=pltpu.SMEM),
                  pl.BlockSpec((1, r, c), lambda h, i: (i, 0, 0))],
        out_specs=pl.BlockSpec((1, 1, r, c), lambda h, i: (h, i, 0, 0)),
        out_shape=jax.ShapeDtypeStruct((N_HEADS, n, r, c), F32),
        compiler_params=_cparams(("arbitrary", "arbitrary")),
        name="bias_tiles",
    )(rel_bias, jnp.asarray(buckets))


def _far_bucket_from(dist_lo):
    b = _t5_bucket_np(np.arange(dist_lo, dist_lo + 4 * MAX_DISTANCE))
    assert (b == NUM_BUCKETS - 1).all()
    return NUM_BUCKETS - 1


def _diff_lambda(lq1_ref, lk1_ref, lq2_ref, lk2_ref, lam_init):
    a = jnp.sum(lq1_ref[...] * lk1_ref[...], axis=1, keepdims=True)
    b = jnp.sum(lq2_ref[...] * lk2_ref[...], axis=1, keepdims=True)
    return jnp.exp(a) - jnp.exp(b) + lam_init


def _split_q(q):
    lane = lax.broadcasted_iota(jnp.int32, q.shape, 1)
    q1 = jnp.where(lane < HEAD_DIM, q, 0.0)
    q2 = jnp.where(lane >= HEAD_DIM, q, 0.0)
    return jnp.concatenate([q1, q2], axis=0).astype(BF16)


def _online_update(s, v_bf, m_ref, l_ref, acc_ref, rows):
    m_old = m_ref[rows]
    m_new = jnp.maximum(m_old, jnp.max(s, axis=1, keepdims=True))
    p = jnp.exp(s - m_new)
    alpha = jnp.exp(m_old - m_new)
    l_ref[rows] = alpha * l_ref[rows] + jnp.sum(p, axis=1, keepdims=True)
    acc_ref[rows] = alpha * acc_ref[rows] + _dot(p.astype(BF16), v_bf)
    m_ref[rows] = m_new


def _diff_finish(acc, l, lam, sw, nq, lam_init):
    o = acc[:nq] / l[:nq] - lam * (acc[nq:] / l[nq:])
    o = o * lax.rsqrt(jnp.mean(o * o, axis=1, keepdims=True) + LN_EPS)
    return o * sw * (1.0 - lam_init)


def _pattn_kernel(q_ref, k_ref, v_ref, bt_ref, lq1, lk1, lq2, lk2, sw_ref, o_ref,
                  kb_ref, vt_ref, m_ref, l_ref, acc_ref, *, qb, nh, lam_init):
    i = pl.program_id(2)
    n_kt = kb_ref.shape[1]

    @pl.when(i == 0)
    def _():
        for g in range(nh):
            cols = slice(g * V_DIM, (g + 1) * V_DIM)
            for c in range(n_kt):
                kb_ref[g, c] = k_ref[0, c * qb:(c + 1) * qb, cols].astype(BF16)
                vt_ref[g, c] = v_ref[0, c * qb:(c + 1) * qb, cols].T.astype(BF16)

    q2 = [_split_q(q_ref[0, :, g * V_DIM:(g + 1) * V_DIM] * (HEAD_DIM ** -0.5)) for g in range(nh)]
    m_ref[...] = jnp.full(m_ref.shape, NEG, F32)
    l_ref[...] = jnp.zeros(l_ref.shape, F32)
    acc_ref[...] = jnp.zeros(acc_ref.shape, F32)

    def tiles(items):
        old = [(m_ref[g], l_ref[g], acc_ref[g]) for g in range(nh)]
        new = []
        for g in range(nh):
            sts = []
            for j, slot in items:
                st = _dot_nt(kb_ref[g, j], q2[g])
                sts.append(st if slot is None else st + bt_ref[g, slot])
            m_old, l_old, acc_old = old[g]
            m_new = m_old
            for st in sts:
                m_new = jnp.maximum(m_new, jnp.max(st, axis=0, keepdims=True))
            alpha = jnp.exp(m_old - m_new)
            l_new = alpha * l_old
            acc_new = alpha * acc_old
            for (j, _), st in zip(items, sts):
                p = jnp.exp(st - m_new)
                l_new = l_new + jnp.sum(p, axis=0, keepdims=True)
                acc_new = acc_new + _dot(vt_ref[g, j], p.astype(BF16))
            new.append((m_new, l_new, acc_new))
        for g in range(nh):
            m_ref[g], l_ref[g], acc_ref[g] = new[g]

    n_far = jnp.maximum(i - 1, 0)

    n_quad = n_far // 4

    def far_quad(jj, carry):
        tiles([(4 * jj + r, None) for r in range(4)])
        return carry

    lax.fori_loop(0, n_quad, far_quad, 0)

    for rest in range(4):
        @pl.when((i >= 1) & (n_far % 4 == rest))
        def _(rest=rest):
            tiles([(4 * n_quad + r, None) for r in range(rest)] + [(i - 1, 1), (i, 0)])

    @pl.when(i == 0)
    def _():
        tiles([(i, 0)])

    lam = _diff_lambda(lq1, lk1, lq2, lk2, lam_init)
    for g in range(nh):
        acc = acc_ref[g]
        l = l_ref[g]
        ot = acc[:, :qb] / l[:, :qb] - lam * (acc[:, qb:] / l[:, qb:])
        ot = ot * lax.rsqrt(jnp.mean(ot * ot, axis=0, keepdims=True) + LN_EPS)
        o_ref[0, :, g * V_DIM:(g + 1) * V_DIM] = ot.T * sw_ref[...] * (1.0 - lam_init)


def _prompt_attention(q, k, v, rel_bias, lq1, lk1, lq2, lk2, subln_w, lam_init, qb, nh):
    b, s, _ = q.shape
    kk = np.arange(qb)[:, None]
    qq = np.tile(np.arange(qb), 2)[None, :]
    buckets = np.stack([_t5_bucket_np(qq - kk), _t5_bucket_np(qb + qq - kk)])
    far_bucket = _far_bucket_from(qb + 1)
    bt = _bias_tiles(rel_bias, buckets, rel_to=far_bucket)
    vec = lambda n: pl.BlockSpec((1, n), lambda bi, h, i: (0, 0))
    gw = nh * V_DIM
    n_kt = s // qb
    return pl.pallas_call(
        functools.partial(_pattn_kernel, qb=qb, nh=nh, lam_init=lam_init),
        grid=(b, N_HEADS // nh, n_kt),
        in_specs=[pl.BlockSpec((1, qb, gw), lambda bi, h, i: (bi, i, h)),
                  pl.BlockSpec((1, s, gw), lambda bi, h, i: (bi, 0, h)),
                  pl.BlockSpec((1, s, gw), lambda bi, h, i: (bi, 0, h)),
                  pl.BlockSpec((nh, 2, qb, 2 * qb), lambda bi, h, i: (h, 0, 0, 0)),
                  vec(HEAD_DIM), vec(HEAD_DIM), vec(HEAD_DIM), vec(HEAD_DIM), vec(V_DIM)],
        out_specs=pl.BlockSpec((1, qb, gw), lambda bi, h, i: (bi, i, h)),
        out_shape=jax.ShapeDtypeStruct((b, s, ATT_WIDTH), F32),
        scratch_shapes=[pltpu.VMEM((nh, n_kt, qb, V_DIM), BF16), pltpu.VMEM((nh, n_kt, V_DIM, qb), BF16),
                        pltpu.VMEM((nh, 1, 2 * qb), F32), pltpu.VMEM((nh, 1, 2 * qb), F32),
                        pltpu.VMEM((nh, V_DIM, 2 * qb), F32)],
        compiler_params=_cparams(("arbitrary", "arbitrary", "arbitrary")),
        name="prompt_attention",
    )(q, k, v, bt, lq1, lk1, lq2, lk2, subln_w)


def _sattn_kernel(pt_ref, q_ref, kn_ref, vn_ref, bt_ref, bn_ref, lq1, lk1, lq2, lk2, sw_ref, *rest,
                  pps, t, lam_init):
    kp = rest[:pps]
    vp = rest[pps:2 * pps]
    o_ref = rest[2 * pps]
    m_ref, l_ref, acc_ref = rest[2 * pps + 1:]
    j = pl.program_id(1)
    nj = pl.num_programs(1)
    nq = 2 * t

    @pl.when(j == 0)
    def _():
        m_ref[...] = jnp.full(m_ref.shape, NEG, F32)
        l_ref[...] = jnp.zeros(l_ref.shape, F32)
        acc_ref[...] = jnp.zeros(acc_ref.shape, F32)

    q = q_ref[0] * (HEAD_DIM ** -0.5)
    m_all, l_all, acc_all = m_ref[...], l_ref[...], acc_ref[...]
    new_m, new_l, new_acc = [], [], []
    for h in range(N_HEADS):
        cols = slice(h * V_DIM, (h + 1) * V_DIM)
        rows = slice(h * nq, (h + 1) * nq)
        q2 = _split_q(q[:, cols])
        s = jnp.concatenate([_dot(q2, kp[r][0, h].reshape(V_DIM, -1).astype(BF16)) for r in range(pps)],
                            axis=1)
        s = s + bt_ref[h, 0]
        m_old = m_all[rows]
        m_new = jnp.maximum(m_old, jnp.max(s, axis=1, keepdims=True))
        p = jnp.exp(s - m_new)
        alpha = jnp.exp(m_old - m_new)
        new_l.append(alpha * l_all[rows] + jnp.sum(p, axis=1, keepdims=True))
        p = p.astype(BF16)
        vrow = pl.ds(h, LANES, stride=N_HEADS)
        pv = _dot(p[:, 0:LANES], vp[0][vrow, :].astype(BF16))
        for r in range(1, pps):
            pv = pv + _dot(p[:, r * LANES:(r + 1) * LANES], vp[r][vrow, :].astype(BF16))
        new_acc.append(alpha * acc_all[rows] + pv)
        new_m.append(m_new)
    m_ref[...] = jnp.concatenate(new_m, axis=0)
    l_ref[...] = jnp.concatenate(new_l, axis=0)
    acc_ref[...] = jnp.concatenate(new_acc, axis=0)

    @pl.when(j == nj - 1)
    def _():
        lam = _diff_lambda(lq1, lk1, lq2, lk2, lam_init)
        for h in range(N_HEADS):
            cols = slice(h * V_DIM, (h + 1) * V_DIM)
            rows = slice(h * nq, (h + 1) * nq)
            q2 = _split_q(q[:, cols])
            s = _dot_nt(q2, kn_ref[0, :, cols].astype(BF16)) + bn_ref[h, 0]
            _online_update(s, vn_ref[0, :, cols].astype(BF16), m_ref, l_ref, acc_ref, rows)
            o_ref[0, :, cols] = _diff_finish(acc_ref[rows], l_ref[rows], lam, sw_ref[...], t, lam_init)


def _sample_attention(q, k_new, v_new, cache_k, cache_v, page_table, rel_bias,
                      lq1, lk1, lq2, lk2, subln_w, lam_init, pps):
    bs, t, _ = q.shape
    ck, cv = cache_k, cache_v
    page = ck.shape[4]
    assert page == LANES
    n_pages = page_table.shape[1]
    past = n_pages * page
    nj = n_pages // pps
    chunk = pps * page
    tq = np.tile(np.arange(t), 2)[:, None]
    kc = np.arange(chunk)[None, :]
    last = _t5_bucket_np(past + tq - (past - chunk + kc))
    assert (_t5_bucket_np(past - (past - chunk) + 1 + np.arange(4 * MAX_DISTANCE)) == NUM_BUCKETS - 1).all()
    far = np.full_like(last, NUM_BUCKETS - 1)
    kn = np.arange(page)[None, :]
    newb = np.where(kn < t, _t5_bucket_np(tq - kn), -1).astype(np.int32)
    bt = _bias_tiles(rel_bias, np.stack([far, last]))
    bn = _bias_tiles(rel_bias, newb[None])
    knp = jnp.pad(k_new, ((0, 0), (0, page - t), (0, 0)))
    vnp = jnp.pad(v_new, ((0, 0), (0, page - t), (0, 0)))
    vec = lambda n: pl.BlockSpec((1, n), lambda b, j, pt: (0, 0))

    def kpage_spec(r):
        return pl.BlockSpec((1,) + ck.shape[1:], lambda b, j, pt: (pt[b, j * pps + r], 0, 0, 0, 0))

    def vpage_spec(r):
        return pl.BlockSpec((page * N_HEADS, V_DIM), lambda b, j, pt: (pt[b, j * pps + r], 0))

    grid_spec = pltpu.PrefetchScalarGridSpec(
        num_scalar_prefetch=1,
        grid=(bs, nj),
        in_specs=[pl.BlockSpec((1, t, QK_WIDTH), lambda b, j, pt: (b, 0, 0)),
                  pl.BlockSpec((1, page, QK_WIDTH), lambda b, j, pt: (b, 0, 0)),
                  pl.BlockSpec((1, page, ATT_WIDTH), lambda b, j, pt: (b, 0, 0)),
                  pl.BlockSpec((N_HEADS, 1, 2 * t, chunk), lambda b, j, pt: (0, (j == nj - 1).astype(jnp.int32), 0, 0)),
                  pl.BlockSpec((N_HEADS, 1, 2 * t, page), lambda b, j, pt: (0, 0, 0, 0)),
                  vec(HEAD_DIM), vec(HEAD_DIM), vec(HEAD_DIM), vec(HEAD_DIM), vec(V_DIM)]
                 + [kpage_spec(r) for r in range(pps)] + [vpage_spec(r) for r in range(pps)],
        out_specs=pl.BlockSpec((1, t, ATT_WIDTH), lambda b, j, pt: (b, 0, 0)),
        scratch_shapes=[pltpu.VMEM((N_HEADS * 2 * t, 1), F32), pltpu.VMEM((N_HEADS * 2 * t, 1), F32),
                        pltpu.VMEM((N_HEADS * 2 * t, V_DIM), F32)],
    )
    return pl.pallas_call(
        functools.partial(_sattn_kernel, pps=pps, t=t, lam_init=lam_init),
        grid_spec=grid_spec,
        out_shape=jax.ShapeDtypeStruct((bs, t, ATT_WIDTH), F32),
        compiler_params=_cparams(("arbitrary", "arbitrary")),
        name="sample_attention",
    )(page_table, q, knp, vnp, bt, bn, lq1, lk1, lq2, lk2, subln_w, *([ck] * pps), *([cv] * pps))


def _layer_norm(y, g, b):
    mu = jnp.mean(y, axis=1, keepdims=True)
    yc = y - mu
    var = jnp.mean(yc * yc, axis=1, keepdims=True)
    return yc * lax.rsqrt(var + LN_EPS) * g + b


def _outproj_kernel(oa_ref, oc_ref, x_ref, mod_ref, w_ref, g_ref, b_ref, x1_ref, h2_ref, *, d, alpha):
    aw = oa_ref.shape[2]
    mix = _dot(oa_ref[0].astype(BF16), w_ref[0:aw, :]) + _dot(oc_ref[0].astype(BF16), w_ref[aw:, :])
    gate1 = mod_ref[0, :, 2 * d:3 * d]
    shift2 = mod_ref[0, :, 3 * d:4 * d]
    scale2 = mod_ref[0, :, 4 * d:5 * d]
    x1 = _layer_norm(alpha * x_ref[0] + gate1 * mix, g_ref[...], b_ref[...])
    x1_ref[0] = x1
    h2_ref[0] = x1 * (1.0 + scale2) + shift2


def _outproj(oa, oc, x, mod, w_out_bf, g, bb, alpha, ts):
    b, s, d = x.shape
    blk = lambda width: pl.BlockSpec((1, ts, width), lambda i, j: (i, j, 0))
    vec = pl.BlockSpec((1, d), lambda i, j: (0, 0))
    return pl.pallas_call(
        functools.partial(_outproj_kernel, d=d, alpha=alpha),
        grid=(b, s // ts),
        in_specs=[blk(oa.shape[2]), blk(oc.shape[2]), blk(d),
                  pl.BlockSpec((1, 1, mod.shape[2]), lambda i, j: (i, 0, 0)),
                  pl.BlockSpec(w_out_bf.shape, lambda i, j: (0, 0)), vec, vec],
        out_specs=[blk(d), blk(d)],
        out_shape=[jax.ShapeDtypeStruct((b, s, d), F32), jax.ShapeDtypeStruct((b, s, d), F32)],
        compiler_params=_cparams(("arbitrary", "arbitrary")),
        name="outproj_ln1",
    )(oa, oc, x, mod, w_out_bf, g, bb)


_CAND_BLOCKS = [(i, 16 if i == 0 else 8, PEER_TOPK // (i + 1)) for i in range(8)]


_STACK_DEPTH = 4
_STACK_SORT = ((0, 1), (2, 3), (1, 2), (0, 1), (2, 3), (1, 2))


def _top16_rows(x, n, v_ref, i_ref):
    sub = lax.broadcasted_iota(jnp.int32, (SUBLANES, x.shape[1]), 0)
    n_grp = n // (SUBLANES * _STACK_DEPTH)
    val, key = [], []
    for g in range(n_grp):
        rows = [(g * _STACK_DEPTH + l) * SUBLANES for l in range(_STACK_DEPTH)]
        sv = [x[r0:r0 + SUBLANES] for r0 in rows]
        sk = [sub + r0 for r0 in rows]
        for a, b in _STACK_SORT:
            swap = sv[b] > sv[a]
            sv[a], sv[b] = jnp.where(swap, sv[b], sv[a]), jnp.where(swap, sv[a], sv[b])
            sk[a], sk[b] = jnp.where(swap, sk[b], sk[a]), jnp.where(swap, sk[a], sk[b])
        val.append(sv)
        key.append(sk)
    for r in range(PEER_TOPK):
        top = val[0][0]
        for g in range(1, n_grp):
            top = jnp.maximum(top, val[g][0])
        m = jnp.max(top, axis=0, keepdims=True)
        cand = jnp.where(val[0][0] == m, key[0][0], n)
        for g in range(1, n_grp):
            cand = jnp.minimum(cand, jnp.where(val[g][0] == m, key[g][0], n))
        idx = jnp.min(cand, axis=0, keepdims=True)
        v_ref[r:r + 1, :] = m
        i_ref[r:r + 1, :] = idx
        if r + 1 < PEER_TOPK:
            for g in range(n_grp):
                hit = key[g][0] == idx
                for l in range(_STACK_DEPTH - 1):
                    val[g][l] = jnp.where(hit, val[g][l + 1], val[g][l])
                    key[g][l] = jnp.where(hit, key[g][l + 1], key[g][l])
                val[g][-1] = jnp.where(hit, -jnp.inf, val[g][-1])


def _route_kernel(h_ref, w_ref, sk_ref, e_ref, g_ref, qt_ref, sv_ref, si_ref, cv_ref, ce_ref, et_ref, gt_ref,
                  *, tb):
    nlb = tb // LANES
    hb = h_ref[...].astype(BF16)
    qt_ref[...] = _dot_nt(w_ref[...], hb)
    half = sk_ref.shape[2]

    def stage1(hp, carry):
        row0 = pl.multiple_of(hp * half, half)
        st = _dot(sk_ref[hp].astype(BF16), qt_ref[pl.ds(row0, half), :].astype(BF16))
        for lb in range(nlb):
            _top16_rows(st[:, lb * LANES:(lb + 1) * LANES], PEER_N_KEYS,
                        sv_ref.at[hp, :, lb * LANES:(lb + 1) * LANES],
                        si_ref.at[hp, :, lb * LANES:(lb + 1) * LANES])
        return carry

    lax.fori_loop(0, 2 * PEER_HEADS, stage1, 0)

    sub8 = lax.broadcasted_iota(jnp.int32, (8, LANES), 0)
    sub16 = lax.broadcasted_iota(jnp.int32, (16, LANES), 0)

    def stage2(h, carry):
        for lb in range(nlb):
            ls = slice(lb * LANES, (lb + 1) * LANES)
            sv0 = sv_ref[2 * h, :, ls]
            sv1 = sv_ref[2 * h + 1, :, ls]
            si0 = si_ref[2 * h, :, ls]
            si1 = si_ref[2 * h + 1, :, ls]
            vals, flats = [], []
            for i, rows, cnt in _CAND_BLOCKS:
                sub = sub16 if rows == 16 else sub8
                vals.append(jnp.where(sub < cnt, sv0[i:i + 1] + sv1[0:rows], -jnp.inf))
                flats.append(i * PEER_TOPK + sub)
            vals.append(sv0[8:16] + sv1[0:1])
            flats.append((sub8 + 8) * PEER_TOPK)
            cand = jnp.concatenate(vals, axis=0)
            flat = jnp.concatenate(flats, axis=0)
            big = PEER_TOPK * PEER_TOPK
            for r in range(PEER_TOPK):
                m = jnp.max(cand, axis=0, keepdims=True)
                fl = jnp.min(jnp.where(cand == m, flat, big), axis=0, keepdims=True)
                cv_ref[r:r + 1, :] = m
                ce_ref[r:r + 1, :] = fl
                cand = jnp.where(flat == fl, -jnp.inf, cand)
            cv = cv_ref[...]
            ex = jnp.exp(cv - cv[0:1])
            g = ex / jnp.sum(ex, axis=0, keepdims=True)
            fl = ce_ref[...]
            fi = fl // PEER_TOPK
            fj = fl - fi * PEER_TOPK
            ea = jnp.zeros_like(fl)
            eb = jnp.zeros_like(fl)
            for i in range(PEER_TOPK):
                ea = jnp.where(fi == i, si0[i:i + 1], ea)
                eb = jnp.where(fj == i, si1[i:i + 1], eb)
            row0 = pl.multiple_of(h * PEER_TOPK, PEER_TOPK)
            gt_ref[pl.ds(row0, PEER_TOPK), ls] = g
            et_ref[pl.ds(row0, PEER_TOPK), ls] = ea * PEER_N_KEYS + eb
        return carry

    lax.fori_loop(0, PEER_HEADS, stage2, 0)
    e_ref[...] = et_ref[...].T
    g_ref[...] = gt_ref[...].T


def _peer_route(h2, w_pq_t_bf, sub_keys, tb):
    t, d = h2.shape
    qw = w_pq_t_bf.shape[0]
    hp, n_keys, half = sub_keys.shape
    slots = PEER_HEADS * PEER_TOPK
    return pl.pallas_call(
        functools.partial(_route_kernel, tb=tb),
        grid=(t // tb,),
        in_specs=[pl.BlockSpec((tb, d), lambda i: (i, 0)),
                  pl.BlockSpec((qw, d), lambda i: (0, 0)),
                  pl.BlockSpec((hp, n_keys, half), lambda i: (0, 0, 0))],
        out_specs=[pl.BlockSpec((tb, slots), lambda i: (i, 0)), pl.BlockSpec((tb, slots), lambda i: (i, 0))],
        out_shape=[jax.ShapeDtypeStruct((t, slots), jnp.int32), jax.ShapeDtypeStruct((t, slots), F32)],
        scratch_shapes=[pltpu.VMEM((qw, tb), F32),
                        pltpu.VMEM((hp, PEER_TOPK, tb), F32), pltpu.VMEM((hp, PEER_TOPK, tb), jnp.int32),
                        pltpu.VMEM((PEER_TOPK, LANES), F32), pltpu.VMEM((PEER_TOPK, LANES), jnp.int32),
                        pltpu.VMEM((slots, tb), jnp.int32), pltpu.VMEM((slots, tb), F32)],
        compiler_params=_cparams(("arbitrary",)),
        name="peer_route",
    )(h2, w_pq_t_bf, sub_keys)


def _expand_kernel(e_ref, g_ref, w_ref, *, tb):
    iota = lax.broadcasted_iota(jnp.int32, (PEER_N_KEYS, e_ref.shape[1]), 0)

    def body(t, carry):
        er = e_ref[pl.ds(t, 1), :]
        gr = g_ref[pl.ds(t, 1), :]
        ea = er // PEER_N_KEYS
        eb = er - ea * PEER_N_KEYS
        oa = jnp.where(iota == ea, gr, 0.0).astype(BF16)
        ob = jnp.where(iota == eb, 1.0, 0.0).astype(BF16)
        w_ref[t] = _dot_nt(oa, ob)
        return carry

    lax.fori_loop(0, tb, body, 0, unroll=32)


def _peer_expand(e, g, tb):
    t, slots = e.shape
    return pl.pallas_call(
        functools.partial(_expand_kernel, tb=tb),
        grid=(t // tb,),
        in_specs=[pl.BlockSpec((tb, slots), lambda i: (i, 0)), pl.BlockSpec((tb, slots), lambda i: (i, 0))],
        out_specs=pl.BlockSpec((tb, PEER_N_KEYS, PEER_N_KEYS), lambda i: (i, 0, 0)),
        out_shape=jax.ShapeDtypeStruct((t, PEER_N_KEYS, PEER_N_KEYS), F32),
        compiler_params=_cparams(("arbitrary",)),
        name="peer_expand",
    )(e, g)


def _dense_kernel(h_ref, ut_ref, v_ref, w_ref, f_ref, hb_ref, *, eb, ec, look):
    j = pl.program_id(1)

    @pl.when(j == 0)
    def _():
        hb_ref[...] = h_ref[...].astype(BF16)
        f_ref[...] = jnp.zeros(f_ref.shape, F32)

    hb = hb_ref[...]
    nc = eb // ec
    pre = lambda c: _dot(hb, ut_ref[:, c * ec:(c + 1) * ec])
    queue = [pre(c) for c in range(min(look, nc))]
    for c in range(nc):
        a = queue.pop(0)
        if c + look < nc:
            queue.append(pre(c + look))
        acc = None
        for s0 in range(0, ec, DENSE_KSLICE):
            parts = []
            for al in range(s0 // PEER_N_KEYS, (s0 + DENSE_KSLICE) // PEER_N_KEYS):
                ai = c * (ec // PEER_N_KEYS) + al
                x = a[:, al * PEER_N_KEYS:(al + 1) * PEER_N_KEYS]
                x = 0.5 * x * (1.0 + lax.erf(x * (2.0 ** -0.5)))
                parts.append((x * w_ref[:, ai, :]).astype(BF16))
            wa = jnp.concatenate(parts, axis=1)
            part = _dot(wa, v_ref[c * ec + s0:c * ec + s0 + DENSE_KSLICE, :])
            acc = part if acc is None else acc + part
        f_ref[...] += acc


def _peer_dense(h2, u_t_bf, v_bf, w, tb, eb, ec):
    t, d = h2.shape
    n_exp = v_bf.shape[0]
    return pl.pallas_call(
        functools.partial(_dense_kernel, eb=eb, ec=ec, look=DENSE_LOOKAHEAD),
        grid=(t // tb, n_exp // eb),
        in_specs=[pl.BlockSpec((tb, d), lambda i, j: (i, 0)),
                  pl.BlockSpec((d, eb), lambda i, j: (0, j)),
                  pl.BlockSpec((eb, d), lambda i, j: (j, 0)),
                  pl.BlockSpec((tb, eb // PEER_N_KEYS, PEER_N_KEYS), lambda i, j: (i, j, 0))],
        out_specs=pl.BlockSpec((tb, d), lambda i, j: (i, 0)),
        out_shape=jax.ShapeDtypeStruct((t, d), F32),
        scratch_shapes=[pltpu.VMEM((tb, d), BF16)],
        compiler_params=_cparams(("arbitrary", "arbitrary")),
        name="peer_dense",
    )(h2, u_t_bf, v_bf, w)


def _final_kernel(x1_ref, f_ref, mod_ref, g_ref, b_ref, y_ref, *, d, alpha):
    gate2 = mod_ref[0, :, 5 * d:6 * d]
    y_ref[0] = _layer_norm(alpha * x1_ref[0] + gate2 * f_ref[0], g_ref[...], b_ref[...])


def _final_ln(x1, f, mod, g, bb, alpha, ts):
    b, s, d = x1.shape
    blk = pl.BlockSpec((1, ts, d), lambda i, j: (i, j, 0))
    vec = pl.BlockSpec((1, d), lambda i, j: (0, 0))
    return pl.pallas_call(
        functools.partial(_final_kernel, d=d, alpha=alpha),
        grid=(b, s // ts),
        in_specs=[blk, blk, pl.BlockSpec((1, 1, mod.shape[2]), lambda i, j: (i, 0, 0)), vec, vec],
        out_specs=blk,
        out_shape=jax.ShapeDtypeStruct((b, s, d), F32),
        compiler_params=_cparams(("arbitrary", "arbitrary")),
        name="final_ln2",
    )(x1, f, mod, g, bb)


def _pick(n, prefs):
    for p in prefs:
        if n % p == 0:
            return p
    return n


def _decoder_layer(x, mod, prefix, attend, p, alpha):
    b, s, d = x.shape
    ts = _pick(s, (512, 256, 128))
    q, k, v, oc, conv_state = _inproj(x, mod, prefix, p["w_in"], p["conv_w"], ts)
    oa = attend(q, k, v)
    x1, h2 = _outproj(oa, oc, x, mod, p["w_out"], p["ln1_g"], p["ln1_b"], alpha, ts)
    t = b * s
    h2f = h2.reshape(t, d)
    e, g = _peer_route(h2f, p["w_pq_t"], p["sub_keys"], _pick(t, (1024, 512, 256, 128)))
    w = _peer_expand(e, g, _pick(t, (128,)))
    n_exp = p["peer_v"].shape[0]
    f = _peer_dense(h2f, p["peer_u_t"], p["peer_v"], w, _pick(t, (512, 256, 128)),
                    _pick(n_exp, (2048,)), 1024)
    y = _final_ln(x1, f.reshape(b, s, d), mod, p["ln2_g"], p["ln2_b"], alpha, ts)
    return y, k, v, conv_state


def kernel(x_prompt, x_sample, cache_k, cache_v, state_conv, page_table, c_prompt, c_sample, rel_bias,
           w_ada, b_ada, w_in, lambda_q1, lambda_k1, lambda_q2, lambda_k2, subln_w, conv_w, w_out,
           ln1_g, ln1_b, w_pq, sub_keys, peer_u, peer_v, ln2_g, ln2_b):
    depth = w_ada.shape[0]
    bp, sp, d = x_prompt.shape
    bs, tsmp, _ = x_sample.shape
    alpha = (2.0 * depth) ** 0.25
    y_p, y_s = x_prompt, x_sample
    outs = [[] for _ in range(6)]
    zero_prefix = jnp.zeros((bp, CONV_K - 1, conv_w.shape[2]), x_prompt.dtype)
    c_all = jnp.concatenate([c_prompt, c_sample], axis=0)
    n_pool, page = cache_k.shape[1], cache_k.shape[2]
    ck_all = jnp.transpose(cache_k, (0, 1, 3, 4, 5, 2)).reshape(depth * n_pool, N_HEADS, 2, HEAD_DIM, page)
    cv_all = cache_v.reshape(depth * n_pool * page * N_HEADS, V_DIM)
    row = lambda a: a.reshape(1, -1)
    for layer in range(depth):
        lam_init = 0.8 - 0.6 * math.exp(-0.3 * layer)
        p = dict(w_in=w_in[layer].astype(BF16), conv_w=conv_w[layer], w_out=w_out[layer].astype(BF16),
                 ln1_g=row(ln1_g[layer]), ln1_b=row(ln1_b[layer]),
                 w_pq_t=w_pq[layer].T.astype(BF16),
                 sub_keys=sub_keys[layer].reshape(2 * PEER_HEADS, PEER_N_KEYS, -1),
                 peer_u_t=peer_u[layer].T.astype(BF16), peer_v=peer_v[layer].astype(BF16),
                 ln2_g=row(ln2_g[layer]), ln2_b=row(ln2_b[layer]))
        lam_args = (row(lambda_q1[layer]), row(lambda_k1[layer]), row(lambda_q2[layer]),
                    row(lambda_k2[layer]), row(subln_w[layer]))
        mod = _adaln(c_all, w_ada[layer], b_ada[layer])[:, None, :]
        attend_p = lambda q, k, v: _prompt_attention(q, k, v, rel_bias, *lam_args, lam_init, 256, 2)
        pt_layer = page_table + layer * n_pool
        attend_s = lambda q, k, v: _sample_attention(q, k, v, ck_all, cv_all, pt_layer,
                                                     rel_bias, *lam_args, lam_init,
                                                     _pick(page_table.shape[1], (16, 8, 4, 2)))
        y_p, kp, vp, cp = _decoder_layer(y_p, mod[:bp], zero_prefix, attend_p, p, alpha)
        y_s, ks, vs, cs = _decoder_layer(y_s, mod[bp:], state_conv[layer], attend_s, p, alpha)
        for lst, val in zip(outs, (kp.reshape(bp, sp, N_HEADS, 2, HEAD_DIM), vp.reshape(bp, sp, N_HEADS, V_DIM), cp,
                                   ks.reshape(bs, tsmp, N_HEADS, 2, HEAD_DIM), vs.reshape(bs, tsmp, N_HEADS, V_DIM), cs)):
            lst.append(val)
    return (y_p, y_s) + tuple(jnp.stack(o) for o in outs)
```

```python
import functools
import math

import numpy as np
import jax
import jax.numpy as jnp
from jax import lax
from jax.experimental import pallas as pl
from jax.experimental.pallas import tpu as pltpu

N_HEADS = 4
HEAD_DIM = 64
V_DIM = 2 * HEAD_DIM
ATT_WIDTH = N_HEADS * V_DIM
QK_WIDTH = N_HEADS * 2 * HEAD_DIM
CONV_K = 3
NUM_BUCKETS = 32
MAX_DISTANCE = 128
PEER_HEADS = 8
PEER_N_KEYS = 128
PEER_TOPK = 16
LN_EPS = 1e-5

LANES = 128
SUBLANES = 8
VMEM_LIMIT_BYTES = 56 * 1024 * 1024
DENSE_LOOKAHEAD = 1
DENSE_KSLICE = 256

NEG = -1e30
BF16 = jnp.bfloat16
F32 = jnp.float32


def _cparams(sem):
    return pltpu.CompilerParams(dimension_semantics=sem, vmem_limit_bytes=VMEM_LIMIT_BYTES)


def _dot(a, b):
    return jnp.dot(a, b, preferred_element_type=F32)


def _dot_nt(a, b):
    return lax.dot_general(a, b, (((1,), (1,)), ((), ())), preferred_element_type=F32)


def _adaln_kernel(c_ref, w_ref, b_ref, o_ref):
    c = c_ref[...]
    s = c * jax.nn.sigmoid(c)
    o_ref[...] = _dot(s.astype(BF16), w_ref[...].astype(BF16)) + b_ref[...]


def _adaln(c, w_ada, b_ada):
    n, d = c.shape
    width = w_ada.shape[1]
    tn = width // 4
    return pl.pallas_call(
        _adaln_kernel,
        grid=(width // tn,),
        in_specs=[pl.BlockSpec((n, d), lambda j: (0, 0)),
                  pl.BlockSpec((d, tn), lambda j: (0, j)),
                  pl.BlockSpec((1, tn), lambda j: (0, j))],
        out_specs=pl.BlockSpec((n, tn), lambda j: (0, j)),
        out_shape=jax.ShapeDtypeStruct((n, width), F32),
        compiler_params=_cparams(("arbitrary",)),
        name="adaln",
    )(c, w_ada, b_ada.reshape(1, width))


def _inproj_kernel(x_ref, mod_ref, pre_ref, w_ref, cw_ref, q_ref, k_ref, v_ref, oc_ref, cs_ref, zbuf,
                   *, ts, d, cw):
    @pl.when(pl.program_id(1) == 0)
    def _():
        zbuf[SUBLANES - 2:SUBLANES, :] = pre_ref[0]

    shift1 = mod_ref[0, :, 0:d]
    scale1 = mod_ref[0, :, d:2 * d]
    h = x_ref[0] * (1.0 + scale1) + shift1
    proj = _dot(h.astype(BF16), w_ref[...])
    o1 = 2 * QK_WIDTH
    o2 = o1 + ATT_WIDTH
    q_ref[0] = proj[:, :QK_WIDTH]
    k_ref[0] = proj[:, QK_WIDTH:o1]
    v_ref[0] = proj[:, o1:o2]
    gb = proj[:, o2:o2 + cw]
    z = proj[:, o2 + cw:o2 + 2 * cw] * proj[:, o2 + 2 * cw:o2 + 3 * cw]
    zbuf[SUBLANES:SUBLANES + ts, :] = z
    y = (cw_ref[0:1, :] * zbuf[SUBLANES - 2:SUBLANES - 2 + ts, :]
         + cw_ref[1:2, :] * zbuf[SUBLANES - 1:SUBLANES - 1 + ts, :]
         + cw_ref[2:3, :] * z)
    oc_ref[0] = gb * y
    tail = zbuf[SUBLANES + ts - 2:SUBLANES + ts, :]
    cs_ref[0] = tail
    zbuf[SUBLANES - 2:SUBLANES, :] = tail


def _inproj(x, mod, prefix, w_in_bf, conv_w, ts):
    b, s, d = x.shape
    cw = conv_w.shape[1]
    pw = w_in_bf.shape[1]
    blk = lambda width: pl.BlockSpec((1, ts, width), lambda i, j: (i, j, 0))
    outs = pl.pallas_call(
        functools.partial(_inproj_kernel, ts=ts, d=d, cw=cw),
        grid=(b, s // ts),
        in_specs=[blk(d),
                  pl.BlockSpec((1, 1, mod.shape[2]), lambda i, j: (i, 0, 0)),
                  pl.BlockSpec((1, CONV_K - 1, cw), lambda i, j: (i, 0, 0)),
                  pl.BlockSpec((d, pw), lambda i, j: (0, 0)),
                  pl.BlockSpec((CONV_K, cw), lambda i, j: (0, 0))],
        out_specs=[blk(QK_WIDTH), blk(QK_WIDTH), blk(ATT_WIDTH), blk(cw),
                   pl.BlockSpec((1, CONV_K - 1, cw), lambda i, j: (i, 0, 0))],
        out_shape=[jax.ShapeDtypeStruct((b, s, QK_WIDTH), F32),
                   jax.ShapeDtypeStruct((b, s, QK_WIDTH), F32),
                   jax.ShapeDtypeStruct((b, s, ATT_WIDTH), F32),
                   jax.ShapeDtypeStruct((b, s, cw), F32),
                   jax.ShapeDtypeStruct((b, CONV_K - 1, cw), F32)],
        scratch_shapes=[pltpu.VMEM((SUBLANES + ts, cw), F32)],
        compiler_params=_cparams(("arbitrary", "arbitrary")),
        name="inproj",
    )(x, mod, prefix, w_in_bf, conv_w)
    return outs


def _t5_bucket_np(dist):
    dist = np.asarray(dist, np.int64)
    n = np.maximum(dist, 0)
    max_exact = NUM_BUCKETS // 2
    n_large = np.maximum(n, max_exact).astype(np.float64)
    large = max_exact + (np.log(n_large / max_exact) / math.log(MAX_DISTANCE / max_exact)
                         * (NUM_BUCKETS - max_exact)).astype(np.int64)
    large = np.minimum(large, NUM_BUCKETS - 1)
    bucket = np.where(n < max_exact, n, large)
    return np.where(dist < 0, -1, bucket).astype(np.int32)


def _bias_kernel(rb_ref, bk_ref, o_ref, *, rel_to):
    h = pl.program_id(0)
    bk = bk_ref[0]
    base = 0.0 if rel_to is None else rb_ref[rel_to, h]
    acc = jnp.where(bk < 0, NEG, 0.0).astype(F32)
    for b in range(NUM_BUCKETS):
        acc = jnp.where(bk == b, rb_ref[b, h] - base, acc)
    o_ref[0, 0] = acc


def _bias_tiles(rel_bias, buckets, rel_to=None):
    n, r, c = buckets.shape
    return pl.pallas_call(
        functools.partial(_bias_kernel, rel_to=rel_to),
        grid=(N_HEADS, n),
        in_specs=[pl.BlockSpec(memory_space=pltpu.SMEM),
                  pl.BlockSpec((1, r, c), lambda h, i: (i, 0, 0))],
        out_specs=pl.BlockSpec((1, 1, r, c), lambda h, i: (h, i, 0, 0)),
        out_shape=jax.ShapeDtypeStruct((N_HEADS, n, r, c), F32),
        compiler_params=_cparams(("arbitrary", "arbitrary")),
        name="bias_tiles",
    )(rel_bias, jnp.asarray(buckets))


def _far_bucket_from(dist_lo):
    b = _t5_bucket_np(np.arange(dist_lo, dist_lo + 4 * MAX_DISTANCE))
    assert (b == NUM_BUCKETS - 1).all()
    return NUM_BUCKETS - 1


def _diff_lambda(lq1_ref, lk1_ref, lq2_ref, lk2_ref, lam_init):
    a = jnp.sum(lq1_ref[...] * lk1_ref[...], axis=1, keepdims=True)
    b = jnp.sum(lq2_ref[...] * lk2_ref[...], axis=1, keepdims=True)
    return jnp.exp(a) - jnp.exp(b) + lam_init


def _split_q(q):
    lane = lax.broadcasted_iota(jnp.int32, q.shape, 1)
    q1 = jnp.where(lane < HEAD_DIM, q, 0.0)
    q2 = jnp.where(lane >= HEAD_DIM, q, 0.0)
    return jnp.concatenate([q1, q2], axis=0).astype(BF16)


def _online_update(s, v_bf, m_ref, l_ref, acc_ref, rows):
    m_old = m_ref[rows]
    m_new = jnp.maximum(m_old, jnp.max(s, axis=1, keepdims=True))
    p = jnp.exp(s - m_new)
    alpha = jnp.exp(m_old - m_new)
    l_ref[rows] = alpha * l_ref[rows] + jnp.sum(p, axis=1, keepdims=True)
    acc_ref[rows] = alpha * acc_ref[rows] + _dot(p.astype(BF16), v_bf)
    m_ref[rows] = m_new


def _diff_finish(acc, l, lam, sw, nq, lam_init):
    o = acc[:nq] / l[:nq] - lam * (acc[nq:] / l[nq:])
    o = o * lax.rsqrt(jnp.mean(o * o, axis=1, keepdims=True) + LN_EPS)
    return o * sw * (1.0 - lam_init)


def _pattn_kernel(q_ref, k_ref, v_ref, bt_ref, lq1, lk1, lq2, lk2, sw_ref, o_ref,
                  kb_ref, vt_ref, m_ref, l_ref, acc_ref, *, qb, nh, lam_init):
    i = pl.program_id(2)
    n_kt = kb_ref.shape[1]

    @pl.when(i == 0)
    def _():
        for g in range(nh):
            cols = slice(g * V_DIM, (g + 1) * V_DIM)
            for c in range(n_kt):
                kb_ref[g, c] = k_ref[0, c * qb:(c + 1) * qb, cols].astype(BF16)
                vt_ref[g, c] = v_ref[0, c * qb:(c + 1) * qb, cols].T.astype(BF16)

    q2 = [_split_q(q_ref[0, :, g * V_DIM:(g + 1) * V_DIM] * (HEAD_DIM ** -0.5)) for g in range(nh)]
    m_ref[...] = jnp.full(m_ref.shape, NEG, F32)
    l_ref[...] = jnp.zeros(l_ref.shape, F32)
    acc_ref[...] = jnp.zeros(acc_ref.shape, F32)

    def tiles(items):
        old = [(m_ref[g], l_ref[g], acc_ref[g]) for g in range(nh)]
        new = []
        for g in range(nh):
            sts = []
            for j, slot in items:
                st = _dot_nt(kb_ref[g, j], q2[g])
                sts.append(st if slot is None else st + bt_ref[g, slot])
            m_old, l_old, acc_old = old[g]
            m_new = m_old
            for st in sts:
                m_new = jnp.maximum(m_new, jnp.max(st, axis=0, keepdims=True))
            alpha = jnp.exp(m_old - m_new)
            l_new = alpha * l_old
            acc_new = alpha * acc_old
            for (j, _), st in zip(items, sts):
                p = jnp.exp(st - m_new)
                l_new = l_new + jnp.sum(p, axis=0, keepdims=True)
                acc_new = acc_new + _dot(vt_ref[g, j], p.astype(BF16))
            new.append((m_new, l_new, acc_new))
        for g in range(nh):
            m_ref[g], l_ref[g], acc_ref[g] = new[g]

    n_far = jnp.maximum(i - 1, 0)

    n_quad = n_far // 4

    def far_quad(jj, carry):
        tiles([(4 * jj + r, None) for r in range(4)])
        return carry

    lax.fori_loop(0, n_quad, far_quad, 0)

    for rest in range(4):
        @pl.when((i >= 1) & (n_far % 4 == rest))
        def _(rest=rest):
            tiles([(4 * n_quad + r, None) for r in range(rest)] + [(i - 1, 1), (i, 0)])

    @pl.when(i == 0)
    def _():
        tiles([(i, 0)])

    lam = _diff_lambda(lq1, lk1, lq2, lk2, lam_init)
    for g in range(nh):
        acc = acc_ref[g]
        l = l_ref[g]
        ot = acc[:, :qb] / l[:, :qb] - lam * (acc[:, qb:] / l[:, qb:])
        ot = ot * lax.rsqrt(jnp.mean(ot * ot, axis=0, keepdims=True) + LN_EPS)
        o_ref[0, :, g * V_DIM:(g + 1) * V_DIM] = ot.T * sw_ref[...] * (1.0 - lam_init)


def _prompt_attention(q, k, v, rel_bias, lq1, lk1, lq2, lk2, subln_w, lam_init, qb, nh):
    b, s, _ = q.shape
    kk = np.arange(qb)[:, None]
    qq = np.tile(np.arange(qb), 2)[None, :]
    buckets = np.stack([_t5_bucket_np(qq - kk), _t5_bucket_np(qb + qq - kk)])
    far_bucket = _far_bucket_from(qb + 1)
    bt = _bias_tiles(rel_bias, buckets, rel_to=far_bucket)
    vec = lambda n: pl.BlockSpec((1, n), lambda bi, h, i: (0, 0))
    gw = nh * V_DIM
    n_kt = s // qb
    return pl.pallas_call(
        functools.partial(_pattn_kernel, qb=qb, nh=nh, lam_init=lam_init),
        grid=(b, N_HEADS // nh, n_kt),
        in_specs=[pl.BlockSpec((1, qb, gw), lambda bi, h, i: (bi, i, h)),
                  pl.BlockSpec((1, s, gw), lambda bi, h, i: (bi, 0, h)),
                  pl.BlockSpec((1, s, gw), lambda bi, h, i: (bi, 0, h)),
                  pl.BlockSpec((nh, 2, qb, 2 * qb), lambda bi, h, i: (h, 0, 0, 0)),
                  vec(HEAD_DIM), vec(HEAD_DIM), vec(HEAD_DIM), vec(HEAD_DIM), vec(V_DIM)],
        out_specs=pl.BlockSpec((1, qb, gw), lambda bi, h, i: (bi, i, h)),
        out_shape=jax.ShapeDtypeStruct((b, s, ATT_WIDTH), F32),
        scratch_shapes=[pltpu.VMEM((nh, n_kt, qb, V_DIM), BF16), pltpu.VMEM((nh, n_kt, V_DIM, qb), BF16),
                        pltpu.VMEM((nh, 1, 2 * qb), F32), pltpu.VMEM((nh, 1, 2 * qb), F32),
                        pltpu.VMEM((nh, V_DIM, 2 * qb), F32)],
        compiler_params=_cparams(("arbitrary", "arbitrary", "arbitrary")),
        name="prompt_attention",
    )(q, k, v, bt, lq1, lk1, lq2, lk2, subln_w)


def _sattn_kernel(pt_ref, q_ref, kn_ref, vn_ref, bt_ref, bn_ref, lq1, lk1, lq2, lk2, sw_ref, *rest,
                  pps, t, lam_init):
    kp = rest[:pps]
    vp = rest[pps:2 * pps]
    o_ref = rest[2 * pps]
    m_ref, l_ref, acc_ref = rest[2 * pps + 1:]
    j = pl.program_id(1)
    nj = pl.num_programs(1)
    nq = 2 * t

    @pl.when(j == 0)
    def _():
        m_ref[...] = jnp.full(m_ref.shape, NEG, F32)
        l_ref[...] = jnp.zeros(l_ref.shape, F32)
        acc_ref[...] = jnp.zeros(acc_ref.shape, F32)

    q = q_ref[0] * (HEAD_DIM ** -0.5)
    m_all, l_all, acc_all = m_ref[...], l_ref[...], acc_ref[...]
    new_m, new_l, new_acc = [], [], []
    for h in range(N_HEADS):
        cols = slice(h * V_DIM, (h + 1) * V_DIM)
        rows = slice(h * nq, (h + 1) * nq)
        q2 = _split_q(q[:, cols])
        s = jnp.concatenate([_dot(q2, kp[r][0, h].reshape(V_DIM, -1).astype(BF16)) for r in range(pps)],
                            axis=1)
        s = s + bt_ref[h, 0]
        m_old = m_all[rows]
        m_new = jnp.maximum(m_old, jnp.max(s, axis=1, keepdims=True))
        p = jnp.exp(s - m_new)
        alpha = jnp.exp(m_old - m_new)
        new_l.append(alpha * l_all[rows] + jnp.sum(p, axis=1, keepdims=True))
        p = p.astype(BF16)
        vrow = pl.ds(h, LANES, stride=N_HEADS)
        pv = _dot(p[:, 0:LANES], vp[0][vrow, :].astype(BF16))
        for r in range(1, pps):
            pv = pv + _dot(p[:, r * LANES:(r + 1) * LANES], vp[r][vrow, :].astype(BF16))
        new_acc.append(alpha * acc_all[rows] + pv)
        new_m.append(m_new)
    m_ref[...] = jnp.concatenate(new_m, axis=0)
    l_ref[...] = jnp.concatenate(new_l, axis=0)
    acc_ref[...] = jnp.concatenate(new_acc, axis=0)

    @pl.when(j == nj - 1)
    def _():
        lam = _diff_lambda(lq1, lk1, lq2, lk2, lam_init)
        for h in range(N_HEADS):
            cols = slice(h * V_DIM, (h + 1) * V_DIM)
            rows = slice(h * nq, (h + 1) * nq)
            q2 = _split_q(q[:, cols])
            s = _dot_nt(q2, kn_ref[0, :, cols].astype(BF16)) + bn_ref[h, 0]
            _online_update(s, vn_ref[0, :, cols].astype(BF16), m_ref, l_ref, acc_ref, rows)
            o_ref[0, :, cols] = _diff_finish(acc_ref[rows], l_ref[rows], lam, sw_ref[...], t, lam_init)


def _sample_attention(q, k_new, v_new, cache_k, cache_v, page_table, rel_bias,
                      lq1, lk1, lq2, lk2, subln_w, lam_init, pps):
    bs, t, _ = q.shape
    ck, cv = cache_k, cache_v
    page = ck.shape[4]
    assert page == LANES
    n_pages = page_table.shape[1]
    past = n_pages * page
    nj = n_pages // pps
    chunk = pps * page
    tq = np.tile(np.arange(t), 2)[:, None]
    kc = np.arange(chunk)[None, :]
    last = _t5_bucket_np(past + tq - (past - chunk + kc))
    assert (_t5_bucket_np(past - (past - chunk) + 1 + np.arange(4 * MAX_DISTANCE)) == NUM_BUCKETS - 1).all()
    far = np.full_like(last, NUM_BUCKETS - 1)
    kn = np.arange(page)[None, :]
    newb = np.where(kn < t, _t5_bucket_np(tq - kn), -1).astype(np.int32)
    bt = _bias_tiles(rel_bias, np.stack([far, last]))
    bn = _bias_tiles(rel_bias, newb[None])
    knp = jnp.pad(k_new, ((0, 0), (0, page - t), (0, 0)))
    vnp = jnp.pad(v_new, ((0, 0), (0, page - t), (0, 0)))
    vec = lambda n: pl.BlockSpec((1, n), lambda b, j, pt: (0, 0))

    def kpage_spec(r):
        return pl.BlockSpec((1,) + ck.shape[1:], lambda b, j, pt: (pt[b, j * pps + r], 0, 0, 0, 0))

    def vpage_spec(r):
        return pl.BlockSpec((page * N_HEADS, V_DIM), lambda b, j, pt: (pt[b, j * pps + r], 0))

    grid_spec = pltpu.PrefetchScalarGridSpec(
        num_scalar_prefetch=1,
        grid=(bs, nj),
        in_specs=[pl.BlockSpec((1, t, QK_WIDTH), lambda b, j, pt: (b, 0, 0)),
                  pl.BlockSpec((1, page, QK_WIDTH), lambda b, j, pt: (b, 0, 0)),
                  pl.BlockSpec((1, page, ATT_WIDTH), lambda b, j, pt: (b, 0, 0)),
                  pl.BlockSpec((N_HEADS, 1, 2 * t, chunk), lambda b, j, pt: (0, (j == nj - 1).astype(jnp.int32), 0, 0)),
                  pl.BlockSpec((N_HEADS, 1, 2 * t, page), lambda b, j, pt: (0, 0, 0, 0)),
                  vec(HEAD_DIM), vec(HEAD_DIM), vec(HEAD_DIM), vec(HEAD_DIM), vec(V_DIM)]
                 + [kpage_spec(r) for r in range(pps)] + [vpage_spec(r) for r in range(pps)],
        out_specs=pl.BlockSpec((1, t, ATT_WIDTH), lambda b, j, pt: (b, 0, 0)),
        scratch_shapes=[pltpu.VMEM((N_HEADS * 2 * t, 1), F32), pltpu.VMEM((N_HEADS * 2 * t, 1), F32),
                        pltpu.VMEM((N_HEADS * 2 * t, V_DIM), F32)],
    )
    return pl.pallas_call(
        functools.partial(_sattn_kernel, pps=pps, t=t, lam_init=lam_init),
        grid_spec=grid_spec,
        out_shape=jax.ShapeDtypeStruct((bs, t, ATT_WIDTH), F32),
        compiler_params=_cparams(("arbitrary", "arbitrary")),
        name="sample_attention",
    )(page_table, q, knp, vnp, bt, bn, lq1, lk1, lq2, lk2, subln_w, *([ck] * pps), *([cv] * pps))


def _layer_norm(y, g, b):
    mu = jnp.mean(y, axis=1, keepdims=True)
    yc = y - mu
    var = jnp.mean(yc * yc, axis=1, keepdims=True)
    return yc * lax.rsqrt(var + LN_EPS) * g + b


def _outproj_kernel(oa_ref, oc_ref, x_ref, mod_ref, w_ref, g_ref, b_ref, x1_ref, h2_ref, *, d, alpha):
    aw = oa_ref.shape[2]
    mix = _dot(oa_ref[0].astype(BF16), w_ref[0:aw, :]) + _dot(oc_ref[0].astype(BF16), w_ref[aw:, :])
    gate1 = mod_ref[0, :, 2 * d:3 * d]
    shift2 = mod_ref[0, :, 3 * d:4 * d]
    scale2 = mod_ref[0, :, 4 * d:5 * d]
    x1 = _layer_norm(alpha * x_ref[0] + gate1 * mix, g_ref[...], b_ref[...])
    x1_ref[0] = x1
    h2_ref[0] = (x1 * (1.0 + scale2) + shift2).astype(h2_ref.dtype)


def _outproj(oa, oc, x, mod, w_out_bf, g, bb, alpha, ts):
    b, s, d = x.shape
    blk = lambda width: pl.BlockSpec((1, ts, width), lambda i, j: (i, j, 0))
    vec = pl.BlockSpec((1, d), lambda i, j: (0, 0))
    return pl.pallas_call(
        functools.partial(_outproj_kernel, d=d, alpha=alpha),
        grid=(b, s // ts),
        in_specs=[blk(oa.shape[2]), blk(oc.shape[2]), blk(d),
                  pl.BlockSpec((1, 1, mod.shape[2]), lambda i, j: (i, 0, 0)),
                  pl.BlockSpec(w_out_bf.shape, lambda i, j: (0, 0)), vec, vec],
        out_specs=[blk(d), blk(d)],
        out_shape=[jax.ShapeDtypeStruct((b, s, d), F32), jax.ShapeDtypeStruct((b, s, d), BF16)],
        compiler_params=_cparams(("arbitrary", "arbitrary")),
        name="outproj_ln1",
    )(oa, oc, x, mod, w_out_bf, g, bb)


_CAND_BLOCKS = [(i, 16 if i == 0 else 8, PEER_TOPK // (i + 1)) for i in range(8)]


_STACK_DEPTH = 4
_STACK_SORT = ((0, 1), (2, 3), (1, 2), (0, 1), (2, 3), (1, 2))


def _top16_rows(x, n, v_ref, i_ref):
    sub = lax.broadcasted_iota(jnp.int32, (SUBLANES, x.shape[1]), 0)
    n_grp = n // (SUBLANES * _STACK_DEPTH)
    val, key = [], []
    for g in range(n_grp):
        rows = [(g * _STACK_DEPTH + l) * SUBLANES for l in range(_STACK_DEPTH)]
        sv = [x[r0:r0 + SUBLANES] for r0 in rows]
        sk = [sub + r0 for r0 in rows]
        for a, b in _STACK_SORT:
            swap = sv[b] > sv[a]
            sv[a], sv[b] = jnp.where(swap, sv[b], sv[a]), jnp.where(swap, sv[a], sv[b])
            sk[a], sk[b] = jnp.where(swap, sk[b], sk[a]), jnp.where(swap, sk[a], sk[b])
        val.append(sv)
        key.append(sk)
    for r in range(PEER_TOPK):
        top = val[0][0]
        for g in range(1, n_grp):
            top = jnp.maximum(top, val[g][0])
        m = jnp.max(top, axis=0, keepdims=True)
        cand = jnp.where(val[0][0] == m, key[0][0], n)
        for g in range(1, n_grp):
            cand = jnp.minimum(cand, jnp.where(val[g][0] == m, key[g][0], n))
        idx = jnp.min(cand, axis=0, keepdims=True)
        v_ref[r:r + 1, :] = m
        i_ref[r:r + 1, :] = idx
        if r + 1 < PEER_TOPK:
            for g in range(n_grp):
                hit = key[g][0] == idx
                for l in range(_STACK_DEPTH - 1):
                    val[g][l] = jnp.where(hit, val[g][l + 1], val[g][l])
                    key[g][l] = jnp.where(hit, key[g][l + 1], key[g][l])
                val[g][-1] = jnp.where(hit, -jnp.inf, val[g][-1])


def _route_kernel(h_ref, w_ref, sk_ref, e_ref, g_ref, qt_ref, sv_ref, si_ref, cv_ref, ce_ref, et_ref, gt_ref,
                  *, tb):
    nlb = tb // LANES
    qt_ref[...] = _dot_nt(w_ref[...], h_ref[...])
    half = sk_ref.shape[2]

    def stage1(hp, carry):
        row0 = pl.multiple_of(hp * half, half)
        st = _dot(sk_ref[hp].astype(BF16), qt_ref[pl.ds(row0, half), :].astype(BF16))
        for lb in range(nlb):
            _top16_rows(st[:, lb * LANES:(lb + 1) * LANES], PEER_N_KEYS,
                        sv_ref.at[hp, :, lb * LANES:(lb + 1) * LANES],
                        si_ref.at[hp, :, lb * LANES:(lb + 1) * LANES])
        return carry

    lax.fori_loop(0, 2 * PEER_HEADS, stage1, 0)

    sub8 = lax.broadcasted_iota(jnp.int32, (8, LANES), 0)
    sub16 = lax.broadcasted_iota(jnp.int32, (16, LANES), 0)

    def stage2(h, carry):
        for lb in range(nlb):
            ls = slice(lb * LANES, (lb + 1) * LANES)
            sv0 = sv_ref[2 * h, :, ls]
            sv1 = sv_ref[2 * h + 1, :, ls]
            si0 = si_ref[2 * h, :, ls]
            si1 = si_ref[2 * h + 1, :, ls]
            vals, flats = [], []
            for i, rows, cnt in _CAND_BLOCKS:
                sub = sub16 if rows == 16 else sub8
                vals.append(jnp.where(sub < cnt, sv0[i:i + 1] + sv1[0:rows], -jnp.inf))
                flats.append(i * PEER_TOPK + sub)
            vals.append(sv0[8:16] + sv1[0:1])
            flats.append((sub8 + 8) * PEER_TOPK)
            cand = jnp.concatenate(vals, axis=0)
            flat = jnp.concatenate(flats, axis=0)
            big = PEER_TOPK * PEER_TOPK
            for r in range(PEER_TOPK):
                m = jnp.max(cand, axis=0, keepdims=True)
                fl = jnp.min(jnp.where(cand == m, flat, big), axis=0, keepdims=True)
                cv_ref[r:r + 1, :] = m
                ce_ref[r:r + 1, :] = fl
                cand = jnp.where(flat == fl, -jnp.inf, cand)
            cv = cv_ref[...]
            ex = jnp.exp(cv - cv[0:1])
            g = ex / jnp.sum(ex, axis=0, keepdims=True)
            fl = ce_ref[...]
            fi = fl // PEER_TOPK
            fj = fl - fi * PEER_TOPK
            ea = jnp.zeros_like(fl)
            eb = jnp.zeros_like(fl)
            for i in range(PEER_TOPK):
                ea = jnp.where(fi == i, si0[i:i + 1], ea)
                eb = jnp.where(fj == i, si1[i:i + 1], eb)
            row0 = pl.multiple_of(h * PEER_TOPK, PEER_TOPK)
            gt_ref[pl.ds(row0, PEER_TOPK), ls] = g
            et_ref[pl.ds(row0, PEER_TOPK), ls] = ea * PEER_N_KEYS + eb
        return carry

    lax.fori_loop(0, PEER_HEADS, stage2, 0)
    e_ref[...] = et_ref[...].T
    g_ref[...] = gt_ref[...].T


def _peer_route(h2, w_pq_t_bf, sub_keys, tb):
    t, d = h2.shape
    qw = w_pq_t_bf.shape[0]
    hp, n_keys, half = sub_keys.shape
    slots = PEER_HEADS * PEER_TOPK
    return pl.pallas_call(
        functools.partial(_route_kernel, tb=tb),
        grid=(t // tb,),
        in_specs=[pl.BlockSpec((tb, d), lambda i: (i, 0)),
                  pl.BlockSpec((qw, d), lambda i: (0, 0)),
                  pl.BlockSpec((hp, n_keys, half), lambda i: (0, 0, 0))],
        out_specs=[pl.BlockSpec((tb, slots), lambda i: (i, 0)), pl.BlockSpec((tb, slots), lambda i: (i, 0))],
        out_shape=[jax.ShapeDtypeStruct((t, slots), jnp.int32), jax.ShapeDtypeStruct((t, slots), F32)],
        scratch_shapes=[pltpu.VMEM((qw, tb), F32),
                        pltpu.VMEM((hp, PEER_TOPK, tb), F32), pltpu.VMEM((hp, PEER_TOPK, tb), jnp.int32),
                        pltpu.VMEM((PEER_TOPK, LANES), F32), pltpu.VMEM((PEER_TOPK, LANES), jnp.int32),
                        pltpu.VMEM((slots, tb), jnp.int32), pltpu.VMEM((slots, tb), F32)],
        compiler_params=_cparams(("arbitrary",)),
        name="peer_route",
    )(h2, w_pq_t_bf, sub_keys)


def _expand_kernel(e_ref, g_ref, w_ref, *, tb):
    iota = lax.broadcasted_iota(jnp.int32, (PEER_N_KEYS, e_ref.shape[1]), 0)

    def body(t, carry):
        er = e_ref[pl.ds(t, 1), :]
        gr = g_ref[pl.ds(t, 1), :]
        ea = er // PEER_N_KEYS
        eb = er - ea * PEER_N_KEYS
        oa = jnp.where(iota == ea, gr, 0.0).astype(BF16)
        ob = jnp.where(iota == eb, 1.0, 0.0).astype(BF16)
        w_ref[t] = _dot_nt(oa, ob)
        return carry

    lax.fori_loop(0, tb, body, 0, unroll=32)


def _peer_expand(e, g, tb):
    t, slots = e.shape
    return pl.pallas_call(
        functools.partial(_expand_kernel, tb=tb),
        grid=(t // tb,),
        in_specs=[pl.BlockSpec((tb, slots), lambda i: (i, 0)), pl.BlockSpec((tb, slots), lambda i: (i, 0))],
        out_specs=pl.BlockSpec((tb, PEER_N_KEYS, PEER_N_KEYS), lambda i: (i, 0, 0)),
        out_shape=jax.ShapeDtypeStruct((t, PEER_N_KEYS, PEER_N_KEYS), F32),
        compiler_params=_cparams(("arbitrary",)),
        name="peer_expand",
    )(e, g)


def _dense_kernel(h_ref, ut_ref, v_ref, w_ref, *rest, eb, ec, look, d, alpha):
    f_ref = rest[-1]
    j = pl.program_id(1)

    @pl.when(j == 0)
    def _():
        f_ref[...] = jnp.zeros(f_ref.shape, F32)

    hb = h_ref[...]
    nc = eb // ec
    pre = lambda c: _dot(hb, ut_ref[:, c * ec:(c + 1) * ec])
    queue = [pre(c) for c in range(min(look, nc))]
    for c in range(nc):
        a = queue.pop(0)
        if c + look < nc:
            queue.append(pre(c + look))
        acc = None
        for s0 in range(0, ec, DENSE_KSLICE):
            parts = []
            for al in range(s0 // PEER_N_KEYS, (s0 + DENSE_KSLICE) // PEER_N_KEYS):
                ai = c * (ec // PEER_N_KEYS) + al
                x = a[:, al * PEER_N_KEYS:(al + 1) * PEER_N_KEYS]
                x = 0.5 * x * (1.0 + lax.erf(x * (2.0 ** -0.5)))
                parts.append((x * w_ref[:, ai, :]).astype(BF16))
            wa = jnp.concatenate(parts, axis=1)
            part = _dot(wa, v_ref[c * ec + s0:c * ec + s0 + DENSE_KSLICE, :])
            acc = part if acc is None else acc + part
        f_ref[...] += acc

    if len(rest) > 1:
        x1_ref, mod_ref, g_ref, b_ref = rest[:4]

        @pl.when(j == pl.num_programs(1) - 1)
        def _():
            gate2 = mod_ref[0, :, 5 * d:6 * d]
            f_ref[...] = _layer_norm(alpha * x1_ref[...] + gate2 * f_ref[...], g_ref[...], b_ref[...])


def _peer_dense(h2, u_t_bf, v_bf, w, tb, eb, ec, ln=None):
    t, d = h2.shape
    n_exp = v_bf.shape[0]
    in_specs = [pl.BlockSpec((tb, d), lambda i, j: (i, 0)),
                pl.BlockSpec((d, eb), lambda i, j: (0, j)),
                pl.BlockSpec((eb, d), lambda i, j: (j, 0)),
                pl.BlockSpec((tb, eb // PEER_N_KEYS, PEER_N_KEYS), lambda i, j: (i, j, 0))]
    args = [h2, u_t_bf, v_bf, w]
    alpha = 0.0
    if ln is not None:
        x1, mod, g, bb, alpha, per_row = ln
        assert per_row % tb == 0
        blocks_per_row = per_row // tb
        in_specs += [pl.BlockSpec((tb, d), lambda i, j: (i, 0)),
                     pl.BlockSpec((1, 1, mod.shape[2]), lambda i, j: (i // blocks_per_row, 0, 0)),
                     pl.BlockSpec((1, d), lambda i, j: (0, 0)), pl.BlockSpec((1, d), lambda i, j: (0, 0))]
        args += [x1, mod, g, bb]
    return pl.pallas_call(
        functools.partial(_dense_kernel, eb=eb, ec=ec, look=DENSE_LOOKAHEAD, d=d, alpha=alpha),
        grid=(t // tb, n_exp // eb),
        in_specs=in_specs,
        out_specs=pl.BlockSpec((tb, d), lambda i, j: (i, 0)),
        out_shape=jax.ShapeDtypeStruct((t, d), F32),
        compiler_params=_cparams(("arbitrary", "arbitrary")),
        name="peer_dense",
    )(*args)


def _final_kernel(x1_ref, f_ref, mod_ref, g_ref, b_ref, y_ref, *, d, alpha):
    gate2 = mod_ref[0, :, 5 * d:6 * d]
    y_ref[0] = _layer_norm(alpha * x1_ref[0] + gate2 * f_ref[0], g_ref[...], b_ref[...])


def _final_ln(x1, f, mod, g, bb, alpha, ts):
    b, s, d = x1.shape
    blk = pl.BlockSpec((1, ts, d), lambda i, j: (i, j, 0))
    vec = pl.BlockSpec((1, d), lambda i, j: (0, 0))
    return pl.pallas_call(
        functools.partial(_final_kernel, d=d, alpha=alpha),
        grid=(b, s // ts),
        in_specs=[blk, blk, pl.BlockSpec((1, 1, mod.shape[2]), lambda i, j: (i, 0, 0)), vec, vec],
        out_specs=blk,
        out_shape=jax.ShapeDtypeStruct((b, s, d), F32),
        compiler_params=_cparams(("arbitrary", "arbitrary")),
        name="final_ln2",
    )(x1, f, mod, g, bb)


def _pick(n, prefs):
    for p in prefs:
        if n % p == 0:
            return p
    return n


def _decoder_layer(x, mod, prefix, attend, p, alpha):
    b, s, d = x.shape
    ts = _pick(s, (512, 256, 128))
    q, k, v, oc, conv_state = _inproj(x, mod, prefix, p["w_in"], p["conv_w"], ts)
    oa = attend(q, k, v)
    x1, h2 = _outproj(oa, oc, x, mod, p["w_out"], p["ln1_g"], p["ln1_b"], alpha, ts)
    t = b * s
    h2f = h2.reshape(t, d)
    e, g = _peer_route(h2f, p["w_pq_t"], p["sub_keys"], _pick(t, (1024, 512, 256, 128)))
    w = _peer_expand(e, g, _pick(t, (128,)))
    n_exp = p["peer_v"].shape[0]
    tb = _pick(t, (512, 256, 128))
    eb = _pick(n_exp, (2048,))
    if s % tb == 0:
        y = _peer_dense(h2f, p["peer_u_t"], p["peer_v"], w, tb, eb, 1024,
                        ln=(x1.reshape(t, d), mod, p["ln2_g"], p["ln2_b"], alpha, s)).reshape(b, s, d)
    else:
        f = _peer_dense(h2f, p["peer_u_t"], p["peer_v"], w, tb, eb, 1024)
        y = _final_ln(x1, f.reshape(b, s, d), mod, p["ln2_g"], p["ln2_b"], alpha, ts)
    return y, k, v, conv_state


def kernel(x_prompt, x_sample, cache_k, cache_v, state_conv, page_table, c_prompt, c_sample, rel_bias,
           w_ada, b_ada, w_in, lambda_q1, lambda_k1, lambda_q2, lambda_k2, subln_w, conv_w, w_out,
           ln1_g, ln1_b, w_pq, sub_keys, peer_u, peer_v, ln2_g, ln2_b):
    depth = w_ada.shape[0]
    bp, sp, d = x_prompt.shape
    bs, tsmp, _ = x_sample.shape
    alpha = (2.0 * depth) ** 0.25
    y_p, y_s = x_prompt, x_sample
    outs = [[] for _ in range(6)]
    zero_prefix = jnp.zeros((bp, CONV_K - 1, conv_w.shape[2]), x_prompt.dtype)
    c_all = jnp.concatenate([c_prompt, c_sample], axis=0)
    n_pool, page = cache_k.shape[1], cache_k.shape[2]
    ck_all = jnp.transpose(cache_k, (0, 1, 3, 4, 5, 2)).reshape(depth * n_pool, N_HEADS, 2, HEAD_DIM, page)
    cv_all = cache_v.reshape(depth * n_pool * page * N_HEADS, V_DIM)
    row = lambda a: a.reshape(1, -1)
    for layer in range(depth):
        lam_init = 0.8 - 0.6 * math.exp(-0.3 * layer)
        p = dict(w_in=w_in[layer].astype(BF16), conv_w=conv_w[layer], w_out=w_out[layer].astype(BF16),
                 ln1_g=row(ln1_g[layer]), ln1_b=row(ln1_b[layer]),
                 w_pq_t=w_pq[layer].T.astype(BF16),
                 sub_keys=sub_keys[layer].reshape(2 * PEER_HEADS, PEER_N_KEYS, -1),
                 peer_u_t=peer_u[layer].T.astype(BF16), peer_v=peer_v[layer].astype(BF16),
                 ln2_g=row(ln2_g[layer]), ln2_b=row(ln2_b[layer]))
        lam_args = (row(lambda_q1[layer]), row(lambda_k1[layer]), row(lambda_q2[layer]),
                    row(lambda_k2[layer]), row(subln_w[layer]))
        mod = _adaln(c_all, w_ada[layer], b_ada[layer])[:, None, :]
        attend_p = lambda q, k, v: _prompt_attention(q, k, v, rel_bias, *lam_args, lam_init, 256, 2)
        pt_layer = page_table + layer * n_pool
        attend_s = lambda q, k, v: _sample_attention(q, k, v, ck_all, cv_all, pt_layer,
                                                     rel_bias, *lam_args, lam_init,
                                                     _pick(page_table.shape[1], (16, 8, 4, 2)))
        y_p, kp, vp, cp = _decoder_layer(y_p, mod[:bp], zero_prefix, attend_p, p, alpha)
        y_s, ks, vs, cs = _decoder_layer(y_s, mod[bp:], state_conv[layer], attend_s, p, alpha)
        for lst, val in zip(outs, (kp.reshape(bp, sp, N_HEADS, 2, HEAD_DIM), vp.reshape(bp, sp, N_HEADS, V_DIM), cp,
                                   ks.reshape(bs, tsmp, N_HEADS, 2, HEAD_DIM), vs.reshape(bs, tsmp, N_HEADS, V_DIM), cs)):
            lst.append(val)
    return (y_p, y_s) + tuple(jnp.stack(o) for o in outs)
```

```python
import functools
import math

import numpy as np
import jax
import jax.numpy as jnp
from jax import lax
from jax.experimental import pallas as pl
from jax.experimental.pallas import tpu as pltpu

N_HEADS = 4
HEAD_DIM = 64
V_DIM = 2 * HEAD_DIM
ATT_WIDTH = N_HEADS * V_DIM
QK_WIDTH = N_HEADS * 2 * HEAD_DIM
CONV_K = 3
NUM_BUCKETS = 32
MAX_DISTANCE = 128
PEER_HEADS = 8
PEER_N_KEYS = 128
PEER_TOPK = 16
LN_EPS = 1e-5

LANES = 128
SUBLANES = 8
VMEM_LIMIT_BYTES = 56 * 1024 * 1024
DENSE_LOOKAHEAD = 1
DENSE_KSLICE = 256

NEG = -1e30
BF16 = jnp.bfloat16
F32 = jnp.float32


def _cparams(sem):
    return pltpu.CompilerParams(dimension_semantics=sem, vmem_limit_bytes=VMEM_LIMIT_BYTES)


def _dot(a, b):
    return jnp.dot(a, b, preferred_element_type=F32)


def _dot_nt(a, b):
    return lax.dot_general(a, b, (((1,), (1,)), ((), ())), preferred_element_type=F32)


def _adaln_kernel(c_ref, w_ref, b_ref, o_ref):
    c = c_ref[...]
    s = c * jax.nn.sigmoid(c)
    o_ref[...] = _dot(s.astype(BF16), w_ref[...].astype(BF16)) + b_ref[...]


def _adaln(c, w_ada, b_ada):
    n, d = c.shape
    width = w_ada.shape[1]
    tn = width // 4
    return pl.pallas_call(
        _adaln_kernel,
        grid=(width // tn,),
        in_specs=[pl.BlockSpec((n, d), lambda j: (0, 0)),
                  pl.BlockSpec((d, tn), lambda j: (0, j)),
                  pl.BlockSpec((1, tn), lambda j: (0, j))],
        out_specs=pl.BlockSpec((n, tn), lambda j: (0, j)),
        out_shape=jax.ShapeDtypeStruct((n, width), F32),
        compiler_params=_cparams(("arbitrary",)),
        name="adaln",
    )(c, w_ada, b_ada.reshape(1, width))


def _inproj_kernel(x_ref, mod_ref, pre_ref, w_ref, cw_ref, q_ref, k_ref, v_ref, oc_ref, cs_ref, zbuf,
                   *, ts, d, cw):
    @pl.when(pl.program_id(1) == 0)
    def _():
        zbuf[SUBLANES - 2:SUBLANES, :] = pre_ref[0]

    shift1 = mod_ref[0, :, 0:d]
    scale1 = mod_ref[0, :, d:2 * d]
    h = x_ref[0] * (1.0 + scale1) + shift1
    proj = _dot(h.astype(BF16), w_ref[...])
    o1 = 2 * QK_WIDTH
    o2 = o1 + ATT_WIDTH
    q_ref[0] = proj[:, :QK_WIDTH]
    k_ref[0] = proj[:, QK_WIDTH:o1]
    v_ref[0] = proj[:, o1:o2]
    gb = proj[:, o2:o2 + cw]
    z = proj[:, o2 + cw:o2 + 2 * cw] * proj[:, o2 + 2 * cw:o2 + 3 * cw]
    zbuf[SUBLANES:SUBLANES + ts, :] = z
    y = (cw_ref[0:1, :] * zbuf[SUBLANES - 2:SUBLANES - 2 + ts, :]
         + cw_ref[1:2, :] * zbuf[SUBLANES - 1:SUBLANES - 1 + ts, :]
         + cw_ref[2:3, :] * z)
    oc_ref[0] = gb * y
    tail = zbuf[SUBLANES + ts - 2:SUBLANES + ts, :]
    cs_ref[0] = tail
    zbuf[SUBLANES - 2:SUBLANES, :] = tail


def _inproj(x, mod, prefix, w_in_bf, conv_w, ts):
    b, s, d = x.shape
    cw = conv_w.shape[1]
    pw = w_in_bf.shape[1]
    blk = lambda width: pl.BlockSpec((1, ts, width), lambda i, j: (i, j, 0))
    outs = pl.pallas_call(
        functools.partial(_inproj_kernel, ts=ts, d=d, cw=cw),
        grid=(b, s // ts),
        in_specs=[blk(d),
                  pl.BlockSpec((1, 1, mod.shape[2]), lambda i, j: (i, 0, 0)),
                  pl.BlockSpec((1, CONV_K - 1, cw), lambda i, j: (i, 0, 0)),
                  pl.BlockSpec((d, pw), lambda i, j: (0, 0)),
                  pl.BlockSpec((CONV_K, cw), lambda i, j: (0, 0))],
        out_specs=[blk(QK_WIDTH), blk(QK_WIDTH), blk(ATT_WIDTH), blk(cw),
                   pl.BlockSpec((1, CONV_K - 1, cw), lambda i, j: (i, 0, 0))],
        out_shape=[jax.ShapeDtypeStruct((b, s, QK_WIDTH), F32),
                   jax.ShapeDtypeStruct((b, s, QK_WIDTH), F32),
                   jax.ShapeDtypeStruct((b, s, ATT_WIDTH), F32),
                   jax.ShapeDtypeStruct((b, s, cw), F32),
                   jax.ShapeDtypeStruct((b, CONV_K - 1, cw), F32)],
        scratch_shapes=[pltpu.VMEM((SUBLANES + ts, cw), F32)],
        compiler_params=_cparams(("arbitrary", "arbitrary")),
        name="inproj",
    )(x, mod, prefix, w_in_bf, conv_w)
    return outs


def _t5_bucket_np(dist):
    dist = np.asarray(dist, np.int64)
    n = np.maximum(dist, 0)
    max_exact = NUM_BUCKETS // 2
    n_large = np.maximum(n, max_exact).astype(np.float64)
    large = max_exact + (np.log(n_large / max_exact) / math.log(MAX_DISTANCE / max_exact)
                         * (NUM_BUCKETS - max_exact)).astype(np.int64)
    large = np.minimum(large, NUM_BUCKETS - 1)
    bucket = np.where(n < max_exact, n, large)
    return np.where(dist < 0, -1, bucket).astype(np.int32)


def _bias_kernel(rb_ref, bk_ref, o_ref, *, rel_to):
    h = pl.program_id(0)
    bk = bk_ref[0]
    base = 0.0 if rel_to is None else rb_ref[rel_to, h]
    acc = jnp.where(bk < 0, NEG, 0.0).astype(F32)
    for b in range(NUM_BUCKETS):
        acc = jnp.where(bk == b, rb_ref[b, h] - base, acc)
    o_ref[0, 0] = acc


def _bias_tiles(rel_bias, buckets, rel_to=None):
    n, r, c = buckets.shape
    return pl.pallas_call(
        functools.partial(_bias_kernel, rel_to=rel_to),
        grid=(N_HEADS, n),
        in_specs=[pl.BlockSpec(memory_space=pltpu.SMEM),
                  pl.BlockSpec((1, r, c), lambda h, i: (i, 0, 0))],
        out_specs=pl.BlockSpec((1, 1, r, c), lambda h, i: (h, i, 0, 0)),
        out_shape=jax.ShapeDtypeStruct((N_HEADS, n, r, c), F32),
        compiler_params=_cparams(("arbitrary", "arbitrary")),
        name="bias_tiles",
    )(rel_bias, jnp.asarray(buckets))


def _far_bucket_from(dist_lo):
    b = _t5_bucket_np(np.arange(dist_lo, dist_lo + 4 * MAX_DISTANCE))
    assert (b == NUM_BUCKETS - 1).all()
    return NUM_BUCKETS - 1


def _diff_lambda(lq1_ref, lk1_ref, lq2_ref, lk2_ref, lam_init):
    a = jnp.sum(lq1_ref[...] * lk1_ref[...], axis=1, keepdims=True)
    b = jnp.sum(lq2_ref[...] * lk2_ref[...], axis=1, keepdims=True)
    return jnp.exp(a) - jnp.exp(b) + lam_init


def _split_q(q):
    lane = lax.broadcasted_iota(jnp.int32, q.shape, 1)
    q1 = jnp.where(lane < HEAD_DIM, q, 0.0)
    q2 = jnp.where(lane >= HEAD_DIM, q, 0.0)
    return jnp.concatenate([q1, q2], axis=0).astype(BF16)


def _online_update(s, v_bf, m_ref, l_ref, acc_ref, rows):
    m_old = m_ref[rows]
    m_new = jnp.maximum(m_old, jnp.max(s, axis=1, keepdims=True))
    p = jnp.exp(s - m_new)
    alpha = jnp.exp(m_old - m_new)
    l_ref[rows] = alpha * l_ref[rows] + jnp.sum(p, axis=1, keepdims=True)
    acc_ref[rows] = alpha * acc_ref[rows] + _dot(p.astype(BF16), v_bf)
    m_ref[rows] = m_new


def _diff_finish(acc, l, lam, sw, nq, lam_init):
    o = acc[:nq] / l[:nq] - lam * (acc[nq:] / l[nq:])
    o = o * lax.rsqrt(jnp.mean(o * o, axis=1, keepdims=True) + LN_EPS)
    return o * sw * (1.0 - lam_init)


def _pattn_kernel(q_ref, k_ref, v_ref, bt_ref, lq1, lk1, lq2, lk2, sw_ref, o_ref,
                  kb_ref, vt_ref, m_ref, l_ref, acc_ref, *, qb, nh, lam_init):
    i = pl.program_id(2)
    n_kt = kb_ref.shape[1]

    @pl.when(i == 0)
    def _():
        for g in range(nh):
            cols = slice(g * V_DIM, (g + 1) * V_DIM)
            for c in range(n_kt):
                kb_ref[g, c] = k_ref[0, c * qb:(c + 1) * qb, cols].astype(BF16)
                vt_ref[g, c] = v_ref[0, c * qb:(c + 1) * qb, cols].T.astype(BF16)

    q2 = [_split_q(q_ref[0, :, g * V_DIM:(g + 1) * V_DIM] * (HEAD_DIM ** -0.5)) for g in range(nh)]
    m_ref[...] = jnp.full(m_ref.shape, NEG, F32)
    l_ref[...] = jnp.zeros(l_ref.shape, F32)
    acc_ref[...] = jnp.zeros(acc_ref.shape, F32)

    def tiles(items):
        old = [(m_ref[g], l_ref[g], acc_ref[g]) for g in range(nh)]
        new = []
        for g in range(nh):
            sts = []
            for j, slot in items:
                st = _dot_nt(kb_ref[g, j], q2[g])
                sts.append(st if slot is None else st + bt_ref[g, slot])
            m_old, l_old, acc_old = old[g]
            m_new = m_old
            for st in sts:
                m_new = jnp.maximum(m_new, jnp.max(st, axis=0, keepdims=True))
            alpha = jnp.exp(m_old - m_new)
            l_new = alpha * l_old
            acc_new = alpha * acc_old
            for (j, _), st in zip(items, sts):
                p = jnp.exp(st - m_new)
                l_new = l_new + jnp.sum(p, axis=0, keepdims=True)
                acc_new = acc_new + _dot(vt_ref[g, j], p.astype(BF16))
            new.append((m_new, l_new, acc_new))
        for g in range(nh):
            m_ref[g], l_ref[g], acc_ref[g] = new[g]

    n_far = jnp.maximum(i - 1, 0)

    n_quad = n_far // 4

    def far_quad(jj, carry):
        tiles([(4 * jj + r, None) for r in range(4)])
        return carry

    lax.fori_loop(0, n_quad, far_quad, 0)

    for rest in range(4):
        @pl.when((i >= 1) & (n_far % 4 == rest))
        def _(rest=rest):
            tiles([(4 * n_quad + r, None) for r in range(rest)] + [(i - 1, 1), (i, 0)])

    @pl.when(i == 0)
    def _():
        tiles([(i, 0)])

    lam = _diff_lambda(lq1, lk1, lq2, lk2, lam_init)
    for g in range(nh):
        acc = acc_ref[g]
        l = l_ref[g]
        ot = acc[:, :qb] / l[:, :qb] - lam * (acc[:, qb:] / l[:, qb:])
        ot = ot * lax.rsqrt(jnp.mean(ot * ot, axis=0, keepdims=True) + LN_EPS)
        o_ref[0, :, g * V_DIM:(g + 1) * V_DIM] = ot.T * sw_ref[...] * (1.0 - lam_init)


def _prompt_attention(q, k, v, rel_bias, lq1, lk1, lq2, lk2, subln_w, lam_init, qb, nh):
    b, s, _ = q.shape
    kk = np.arange(qb)[:, None]
    qq = np.tile(np.arange(qb), 2)[None, :]
    buckets = np.stack([_t5_bucket_np(qq - kk), _t5_bucket_np(qb + qq - kk)])
    far_bucket = _far_bucket_from(qb + 1)
    bt = _bias_tiles(rel_bias, buckets, rel_to=far_bucket)
    vec = lambda n: pl.BlockSpec((1, n), lambda bi, h, i: (0, 0))
    gw = nh * V_DIM
    n_kt = s // qb
    return pl.pallas_call(
        functools.partial(_pattn_kernel, qb=qb, nh=nh, lam_init=lam_init),
        grid=(b, N_HEADS // nh, n_kt),
        in_specs=[pl.BlockSpec((1, qb, gw), lambda bi, h, i: (bi, i, h)),
                  pl.BlockSpec((1, s, gw), lambda bi, h, i: (bi, 0, h)),
                  pl.BlockSpec((1, s, gw), lambda bi, h, i: (bi, 0, h)),
                  pl.BlockSpec((nh, 2, qb, 2 * qb), lambda bi, h, i: (h, 0, 0, 0)),
                  vec(HEAD_DIM), vec(HEAD_DIM), vec(HEAD_DIM), vec(HEAD_DIM), vec(V_DIM)],
        out_specs=pl.BlockSpec((1, qb, gw), lambda bi, h, i: (bi, i, h)),
        out_shape=jax.ShapeDtypeStruct((b, s, ATT_WIDTH), F32),
        scratch_shapes=[pltpu.VMEM((nh, n_kt, qb, V_DIM), BF16), pltpu.VMEM((nh, n_kt, V_DIM, qb), BF16),
                        pltpu.VMEM((nh, 1, 2 * qb), F32), pltpu.VMEM((nh, 1, 2 * qb), F32),
                        pltpu.VMEM((nh, V_DIM, 2 * qb), F32)],
        compiler_params=_cparams(("arbitrary", "arbitrary", "arbitrary")),
        name="prompt_attention",
    )(q, k, v, bt, lq1, lk1, lq2, lk2, subln_w)


def _sattn_kernel(pt_ref, q_ref, kn_ref, vn_ref, bt_ref, bn_ref, lq1, lk1, lq2, lk2, sw_ref, *rest,
                  pps, t, lam_init):
    kp = rest[:pps]
    vp = rest[pps:2 * pps]
    o_ref = rest[2 * pps]
    m_ref, l_ref, acc_ref = rest[2 * pps + 1:]
    j = pl.program_id(1)
    nj = pl.num_programs(1)
    nq = 2 * t

    @pl.when(j == 0)
    def _():
        m_ref[...] = jnp.full(m_ref.shape, NEG, F32)
        l_ref[...] = jnp.zeros(l_ref.shape, F32)
        acc_ref[...] = jnp.zeros(acc_ref.shape, F32)

    q = q_ref[0] * (HEAD_DIM ** -0.5)
    m_all, l_all, acc_all = m_ref[...], l_ref[...], acc_ref[...]
    new_m, new_l, new_acc = [], [], []
    for h in range(N_HEADS):
        cols = slice(h * V_DIM, (h + 1) * V_DIM)
        rows = slice(h * nq, (h + 1) * nq)
        q2 = _split_q(q[:, cols])
        s = jnp.concatenate([_dot(q2, kp[r][0, h].reshape(V_DIM, -1).astype(BF16)) for r in range(pps)],
                            axis=1)
        s = s + bt_ref[h, 0]
        m_old = m_all[rows]
        m_new = jnp.maximum(m_old, jnp.max(s, axis=1, keepdims=True))
        p = jnp.exp(s - m_new)
        alpha = jnp.exp(m_old - m_new)
        new_l.append(alpha * l_all[rows] + jnp.sum(p, axis=1, keepdims=True))
        p = p.astype(BF16)
        vrow = pl.ds(h, LANES, stride=N_HEADS)
        pv = _dot(p[:, 0:LANES], vp[0][vrow, :].astype(BF16))
        for r in range(1, pps):
            pv = pv + _dot(p[:, r * LANES:(r + 1) * LANES], vp[r][vrow, :].astype(BF16))
        new_acc.append(alpha * acc_all[rows] + pv)
        new_m.append(m_new)
    m_ref[...] = jnp.concatenate(new_m, axis=0)
    l_ref[...] = jnp.concatenate(new_l, axis=0)
    acc_ref[...] = jnp.concatenate(new_acc, axis=0)

    @pl.when(j == nj - 1)
    def _():
        lam = _diff_lambda(lq1, lk1, lq2, lk2, lam_init)
        for h in range(N_HEADS):
            cols = slice(h * V_DIM, (h + 1) * V_DIM)
            rows = slice(h * nq, (h + 1) * nq)
            q2 = _split_q(q[:, cols])
            s = _dot_nt(q2, kn_ref[0, :, cols].astype(BF16)) + bn_ref[h, 0]
            _online_update(s, vn_ref[0, :, cols].astype(BF16), m_ref, l_ref, acc_ref, rows)
            o_ref[0, :, cols] = _diff_finish(acc_ref[rows], l_ref[rows], lam, sw_ref[...], t, lam_init)


def _sample_attention(q, k_new, v_new, cache_k, cache_v, page_table, rel_bias,
                      lq1, lk1, lq2, lk2, subln_w, lam_init, pps):
    bs, t, _ = q.shape
    ck, cv = cache_k, cache_v
    page = ck.shape[4]
    assert page == LANES
    n_pages = page_table.shape[1]
    past = n_pages * page
    nj = n_pages // pps
    chunk = pps * page
    tq = np.tile(np.arange(t), 2)[:, None]
    kc = np.arange(chunk)[None, :]
    last = _t5_bucket_np(past + tq - (past - chunk + kc))
    assert (_t5_bucket_np(past - (past - chunk) + 1 + np.arange(4 * MAX_DISTANCE)) == NUM_BUCKETS - 1).all()
    far = np.full_like(last, NUM_BUCKETS - 1)
    kn = np.arange(page)[None, :]
    newb = np.where(kn < t, _t5_bucket_np(tq - kn), -1).astype(np.int32)
    bt = _bias_tiles(rel_bias, np.stack([far, last]))
    bn = _bias_tiles(rel_bias, newb[None])
    knp = jnp.pad(k_new, ((0, 0), (0, page - t), (0, 0)))
    vnp = jnp.pad(v_new, ((0, 0), (0, page - t), (0, 0)))
    vec = lambda n: pl.BlockSpec((1, n), lambda b, j, pt: (0, 0))

    def kpage_spec(r):
        return pl.BlockSpec((1,) + ck.shape[1:], lambda b, j, pt: (pt[b, j * pps + r], 0, 0, 0, 0))

    def vpage_spec(r):
        return pl.BlockSpec((page * N_HEADS, V_DIM), lambda b, j, pt: (pt[b, j * pps + r], 0))

    grid_spec = pltpu.PrefetchScalarGridSpec(
        num_scalar_prefetch=1,
        grid=(bs, nj),
        in_specs=[pl.BlockSpec((1, t, QK_WIDTH), lambda b, j, pt: (b, 0, 0)),
                  pl.BlockSpec((1, page, QK_WIDTH), lambda b, j, pt: (b, 0, 0)),
                  pl.BlockSpec((1, page, ATT_WIDTH), lambda b, j, pt: (b, 0, 0)),
                  pl.BlockSpec((N_HEADS, 1, 2 * t, chunk), lambda b, j, pt: (0, (j == nj - 1).astype(jnp.int32), 0, 0)),
                  pl.BlockSpec((N_HEADS, 1, 2 * t, page), lambda b, j, pt: (0, 0, 0, 0)),
                  vec(HEAD_DIM), vec(HEAD_DIM), vec(HEAD_DIM), vec(HEAD_DIM), vec(V_DIM)]
                 + [kpage_spec(r) for r in range(pps)] + [vpage_spec(r) for r in range(pps)],
        out_specs=pl.BlockSpec((1, t, ATT_WIDTH), lambda b, j, pt: (b, 0, 0)),
        scratch_shapes=[pltpu.VMEM((N_HEADS * 2 * t, 1), F32), pltpu.VMEM((N_HEADS * 2 * t, 1), F32),
                        pltpu.VMEM((N_HEADS * 2 * t, V_DIM), F32)],
    )
    return pl.pallas_call(
        functools.partial(_sattn_kernel, pps=pps, t=t, lam_init=lam_init),
        grid_spec=grid_spec,
        out_shape=jax.ShapeDtypeStruct((bs, t, ATT_WIDTH), F32),
        compiler_params=_cparams(("arbitrary", "arbitrary")),
        name="sample_attention",
    )(page_table, q, knp, vnp, bt, bn, lq1, lk1, lq2, lk2, subln_w, *([ck] * pps), *([cv] * pps))


def _layer_norm(y, g, b):
    mu = jnp.mean(y, axis=1, keepdims=True)
    yc = y - mu
    var = jnp.mean(yc * yc, axis=1, keepdims=True)
    return yc * lax.rsqrt(var + LN_EPS) * g + b


def _outproj_kernel(oa_ref, oc_ref, x_ref, mod_ref, w_ref, g_ref, b_ref, x1_ref, h2_ref, *, d, alpha):
    aw = oa_ref.shape[2]
    mix = _dot(oa_ref[0].astype(BF16), w_ref[0:aw, :]) + _dot(oc_ref[0].astype(BF16), w_ref[aw:, :])
    gate1 = mod_ref[0, :, 2 * d:3 * d]
    shift2 = mod_ref[0, :, 3 * d:4 * d]
    scale2 = mod_ref[0, :, 4 * d:5 * d]
    x1 = _layer_norm(alpha * x_ref[0] + gate1 * mix, g_ref[...], b_ref[...])
    x1_ref[0] = x1
    h2_ref[0] = (x1 * (1.0 + scale2) + shift2).astype(h2_ref.dtype)


def _outproj(oa, oc, x, mod, w_out_bf, g, bb, alpha, ts):
    b, s, d = x.shape
    blk = lambda width: pl.BlockSpec((1, ts, width), lambda i, j: (i, j, 0))
    vec = pl.BlockSpec((1, d), lambda i, j: (0, 0))
    return pl.pallas_call(
        functools.partial(_outproj_kernel, d=d, alpha=alpha),
        grid=(b, s // ts),
        in_specs=[blk(oa.shape[2]), blk(oc.shape[2]), blk(d),
                  pl.BlockSpec((1, 1, mod.shape[2]), lambda i, j: (i, 0, 0)),
                  pl.BlockSpec(w_out_bf.shape, lambda i, j: (0, 0)), vec, vec],
        out_specs=[blk(d), blk(d)],
        out_shape=[jax.ShapeDtypeStruct((b, s, d), F32), jax.ShapeDtypeStruct((b, s, d), BF16)],
        compiler_params=_cparams(("arbitrary", "arbitrary")),
        name="outproj_ln1",
    )(oa, oc, x, mod, w_out_bf, g, bb)


_CAND_BLOCKS = [(i, 16 if i == 0 else 8, PEER_TOPK // (i + 1)) for i in range(8)]


_STACK_DEPTH = 4
_STACK_SORT = ((0, 1), (2, 3), (1, 2), (0, 1), (2, 3), (1, 2))


def _top16_rows(x, n, v_ref, i_ref):
    sub = lax.broadcasted_iota(jnp.int32, (SUBLANES, x.shape[1]), 0)
    n_grp = n // (SUBLANES * _STACK_DEPTH)
    val, key = [], []
    for g in range(n_grp):
        rows = [(g * _STACK_DEPTH + l) * SUBLANES for l in range(_STACK_DEPTH)]
        sv = [x[r0:r0 + SUBLANES] for r0 in rows]
        sk = [sub + r0 for r0 in rows]
        for a, b in _STACK_SORT:
            swap = sv[b] > sv[a]
            sv[a], sv[b] = jnp.where(swap, sv[b], sv[a]), jnp.where(swap, sv[a], sv[b])
            sk[a], sk[b] = jnp.where(swap, sk[b], sk[a]), jnp.where(swap, sk[a], sk[b])
        val.append(sv)
        key.append(sk)
    for r in range(PEER_TOPK):
        top = val[0][0]
        for g in range(1, n_grp):
            top = jnp.maximum(top, val[g][0])
        m = jnp.max(top, axis=0, keepdims=True)
        cand = jnp.where(val[0][0] == m, key[0][0], n)
        for g in range(1, n_grp):
            cand = jnp.minimum(cand, jnp.where(val[g][0] == m, key[g][0], n))
        idx = jnp.min(cand, axis=0, keepdims=True)
        v_ref[r:r + 1, :] = m
        i_ref[r:r + 1, :] = idx
        if r + 1 < PEER_TOPK:
            for g in range(n_grp):
                hit = key[g][0] == idx
                for l in range(_STACK_DEPTH - 1):
                    val[g][l] = jnp.where(hit, val[g][l + 1], val[g][l])
                    key[g][l] = jnp.where(hit, key[g][l + 1], key[g][l])
                val[g][-1] = jnp.where(hit, -jnp.inf, val[g][-1])


def _route_kernel(h_ref, w_ref, sk_ref, e_ref, g_ref, qt_ref, sv_ref, si_ref, cv_ref, ce_ref, et_ref, gt_ref,
                  *, tb):
    nlb = tb // LANES
    qt_ref[...] = _dot_nt(w_ref[...], h_ref[...])
    half = sk_ref.shape[2]

    def stage1(hp, carry):
        row0 = pl.multiple_of(hp * half, half)
        st = _dot(sk_ref[hp].astype(BF16), qt_ref[pl.ds(row0, half), :].astype(BF16))
        for lb in range(nlb):
            _top16_rows(st[:, lb * LANES:(lb + 1) * LANES], PEER_N_KEYS,
                        sv_ref.at[hp, :, lb * LANES:(lb + 1) * LANES],
                        si_ref.at[hp, :, lb * LANES:(lb + 1) * LANES])
        return carry

    lax.fori_loop(0, 2 * PEER_HEADS, stage1, 0)

    sub8 = lax.broadcasted_iota(jnp.int32, (8, LANES), 0)
    sub16 = lax.broadcasted_iota(jnp.int32, (16, LANES), 0)

    def stage2(h, carry):
        for lb in range(nlb):
            ls = slice(lb * LANES, (lb + 1) * LANES)
            sv0 = sv_ref[2 * h, :, ls]
            sv1 = sv_ref[2 * h + 1, :, ls]
            si0 = si_ref[2 * h, :, ls]
            si1 = si_ref[2 * h + 1, :, ls]
            vals, flats = [], []
            for i, rows, cnt in _CAND_BLOCKS:
                sub = sub16 if rows == 16 else sub8
                vals.append(jnp.where(sub < cnt, sv0[i:i + 1] + sv1[0:rows], -jnp.inf))
                flats.append(i * PEER_TOPK + sub)
            vals.append(sv0[8:16] + sv1[0:1])
            flats.append((sub8 + 8) * PEER_TOPK)
            cand = jnp.concatenate(vals, axis=0)
            flat = jnp.concatenate(flats, axis=0)
            big = PEER_TOPK * PEER_TOPK
            for r in range(PEER_TOPK):
                m = jnp.max(cand, axis=0, keepdims=True)
                fl = jnp.min(jnp.where(cand == m, flat, big), axis=0, keepdims=True)
                cv_ref[r:r + 1, :] = m
                ce_ref[r:r + 1, :] = fl
                cand = jnp.where(flat == fl, -jnp.inf, cand)
            cv = cv_ref[...]
            ex = jnp.exp(cv - cv[0:1])
            g = ex / jnp.sum(ex, axis=0, keepdims=True)
            fl = ce_ref[...]
            fi = fl // PEER_TOPK
            fj = fl - fi * PEER_TOPK
            ea = jnp.zeros_like(fl)
            eb = jnp.zeros_like(fl)
            for i in range(PEER_TOPK):
                ea = jnp.where(fi == i, si0[i:i + 1], ea)
                eb = jnp.where(fj == i, si1[i:i + 1], eb)
            row0 = pl.multiple_of(h * PEER_TOPK, PEER_TOPK)
            gt_ref[pl.ds(row0, PEER_TOPK), ls] = g
            et_ref[pl.ds(row0, PEER_TOPK), ls] = ea * PEER_N_KEYS + eb
        return carry

    lax.fori_loop(0, PEER_HEADS, stage2, 0)
    e_ref[...] = et_ref[...].T
    g_ref[...] = gt_ref[...].T


def _peer_route(h2, w_pq_t_bf, sub_keys, tb):
    t, d = h2.shape
    qw = w_pq_t_bf.shape[0]
    hp, n_keys, half = sub_keys.shape
    slots = PEER_HEADS * PEER_TOPK
    return pl.pallas_call(
        functools.partial(_route_kernel, tb=tb),
        grid=(t // tb,),
        in_specs=[pl.BlockSpec((tb, d), lambda i: (i, 0)),
                  pl.BlockSpec((qw, d), lambda i: (0, 0)),
                  pl.BlockSpec((hp, n_keys, half), lambda i: (0, 0, 0))],
        out_specs=[pl.BlockSpec((tb, slots), lambda i: (i, 0)), pl.BlockSpec((tb, slots), lambda i: (i, 0))],
        out_shape=[jax.ShapeDtypeStruct((t, slots), jnp.int32), jax.ShapeDtypeStruct((t, slots), F32)],
        scratch_shapes=[pltpu.VMEM((qw, tb), F32),
                        pltpu.VMEM((hp, PEER_TOPK, tb), F32), pltpu.VMEM((hp, PEER_TOPK, tb), jnp.int32),
                        pltpu.VMEM((PEER_TOPK, LANES), F32), pltpu.VMEM((PEER_TOPK, LANES), jnp.int32),
                        pltpu.VMEM((slots, tb), jnp.int32), pltpu.VMEM((slots, tb), F32)],
        compiler_params=_cparams(("arbitrary",)),
        name="peer_route",
    )(h2, w_pq_t_bf, sub_keys)


def _expand_kernel(e_ref, g_ref, w_ref, *, tb):
    iota = lax.broadcasted_iota(jnp.int32, (PEER_N_KEYS, e_ref.shape[1]), 0)

    def body(t, carry):
        er = e_ref[pl.ds(t, 1), :]
        gr = g_ref[pl.ds(t, 1), :]
        ea = er // PEER_N_KEYS
        eb = er - ea * PEER_N_KEYS
        oa = jnp.where(iota == ea, gr, 0.0).astype(BF16)
        ob = jnp.where(iota == eb, 1.0, 0.0).astype(BF16)
        w_ref[t] = _dot_nt(oa, ob).astype(w_ref.dtype)
        return carry

    lax.fori_loop(0, tb, body, 0, unroll=32)


def _peer_expand(e, g, tb):
    t, slots = e.shape
    return pl.pallas_call(
        functools.partial(_expand_kernel, tb=tb),
        grid=(t // tb,),
        in_specs=[pl.BlockSpec((tb, slots), lambda i: (i, 0)), pl.BlockSpec((tb, slots), lambda i: (i, 0))],
        out_specs=pl.BlockSpec((tb, PEER_N_KEYS, PEER_N_KEYS), lambda i: (i, 0, 0)),
        out_shape=jax.ShapeDtypeStruct((t, PEER_N_KEYS, PEER_N_KEYS), BF16),
        compiler_params=_cparams(("arbitrary",)),
        name="peer_expand",
    )(e, g)


def _dense_kernel(h_ref, ut_ref, v_ref, w_ref, *rest, eb, ec, look, d, alpha):
    f_ref, wf_ref = rest[-2:]
    rest = rest[:-1]
    j = pl.program_id(1)

    @pl.when(j == 0)
    def _():
        f_ref[...] = jnp.zeros(f_ref.shape, F32)

    wf_ref[...] = w_ref[...].astype(F32)

    hb = h_ref[...]
    nc = eb // ec
    pre = lambda c: _dot(hb, ut_ref[:, c * ec:(c + 1) * ec])
    queue = [pre(c) for c in range(min(look, nc))]
    for c in range(nc):
        a = queue.pop(0)
        if c + look < nc:
            queue.append(pre(c + look))
        acc = None
        for s0 in range(0, ec, DENSE_KSLICE):
            parts = []
            for al in range(s0 // PEER_N_KEYS, (s0 + DENSE_KSLICE) // PEER_N_KEYS):
                ai = c * (ec // PEER_N_KEYS) + al
                x = a[:, al * PEER_N_KEYS:(al + 1) * PEER_N_KEYS]
                x = 0.5 * x * (1.0 + lax.erf(x * (2.0 ** -0.5)))
                parts.append((x * wf_ref[:, ai, :]).astype(BF16))
            wa = jnp.concatenate(parts, axis=1)
            part = _dot(wa, v_ref[c * ec + s0:c * ec + s0 + DENSE_KSLICE, :])
            acc = part if acc is None else acc + part
        f_ref[...] += acc

    if len(rest) > 1:
        x1_ref, mod_ref, g_ref, b_ref = rest[:4]

        @pl.when(j == pl.num_programs(1) - 1)
        def _():
            gate2 = mod_ref[0, :, 5 * d:6 * d]
            f_ref[...] = _layer_norm(alpha * x1_ref[...] + gate2 * f_ref[...], g_ref[...], b_ref[...])


def _peer_dense(h2, u_t_bf, v_bf, w, tb, eb, ec, ln=None):
    t, d = h2.shape
    n_exp = v_bf.shape[0]
    in_specs = [pl.BlockSpec((tb, d), lambda i, j: (i, 0)),
                pl.BlockSpec((d, eb), lambda i, j: (0, j)),
                pl.BlockSpec((eb, d), lambda i, j: (j, 0)),
                pl.BlockSpec((tb, eb // PEER_N_KEYS, PEER_N_KEYS), lambda i, j: (i, j, 0))]
    args = [h2, u_t_bf, v_bf, w]
    alpha = 0.0
    if ln is not None:
        x1, mod, g, bb, alpha, per_row = ln
        assert per_row % tb == 0
        blocks_per_row = per_row // tb
        in_specs += [pl.BlockSpec((tb, d), lambda i, j: (i, 0)),
                     pl.BlockSpec((1, 1, mod.shape[2]), lambda i, j: (i // blocks_per_row, 0, 0)),
                     pl.BlockSpec((1, d), lambda i, j: (0, 0)), pl.BlockSpec((1, d), lambda i, j: (0, 0))]
        args += [x1, mod, g, bb]
    return pl.pallas_call(
        functools.partial(_dense_kernel, eb=eb, ec=ec, look=DENSE_LOOKAHEAD, d=d, alpha=alpha),
        grid=(t // tb, n_exp // eb),
        in_specs=in_specs,
        out_specs=pl.BlockSpec((tb, d), lambda i, j: (i, 0)),
        out_shape=jax.ShapeDtypeStruct((t, d), F32),
        scratch_shapes=[pltpu.VMEM((tb, eb // PEER_N_KEYS, PEER_N_KEYS), F32)],
        compiler_params=_cparams(("arbitrary", "arbitrary")),
        name="peer_dense",
    )(*args)


def _final_kernel(x1_ref, f_ref, mod_ref, g_ref, b_ref, y_ref, *, d, alpha):
    gate2 = mod_ref[0, :, 5 * d:6 * d]
    y_ref[0] = _layer_norm(alpha * x1_ref[0] + gate2 * f_ref[0], g_ref[...], b_ref[...])


def _final_ln(x1, f, mod, g, bb, alpha, ts):
    b, s, d = x1.shape
    blk = pl.BlockSpec((1, ts, d), lambda i, j: (i, j, 0))
    vec = pl.BlockSpec((1, d), lambda i, j: (0, 0))
    return pl.pallas_call(
        functools.partial(_final_kernel, d=d, alpha=alpha),
        grid=(b, s // ts),
        in_specs=[blk, blk, pl.BlockSpec((1, 1, mod.shape[2]), lambda i, j: (i, 0, 0)), vec, vec],
        out_specs=blk,
        out_shape=jax.ShapeDtypeStruct((b, s, d), F32),
        compiler_params=_cparams(("arbitrary", "arbitrary")),
        name="final_ln2",
    )(x1, f, mod, g, bb)


def _pick(n, prefs):
    for p in prefs:
        if n % p == 0:
            return p
    return n


def _decoder_layer(x, mod, prefix, attend, p, alpha):
    b, s, d = x.shape
    ts = _pick(s, (512, 256, 128))
    q, k, v, oc, conv_state = _inproj(x, mod, prefix, p["w_in"], p["conv_w"], ts)
    oa = attend(q, k, v)
    x1, h2 = _outproj(oa, oc, x, mod, p["w_out"], p["ln1_g"], p["ln1_b"], alpha, ts)
    t = b * s
    h2f = h2.reshape(t, d)
    e, g = _peer_route(h2f, p["w_pq_t"], p["sub_keys"], _pick(t, (1024, 512, 256, 128)))
    w = _peer_expand(e, g, _pick(t, (128,)))
    n_exp = p["peer_v"].shape[0]
    tb = _pick(t, (512, 256, 128))
    eb = _pick(n_exp, (2048,))
    if s % tb == 0:
        y = _peer_dense(h2f, p["peer_u_t"], p["peer_v"], w, tb, eb, 1024,
                        ln=(x1.reshape(t, d), mod, p["ln2_g"], p["ln2_b"], alpha, s)).reshape(b, s, d)
    else:
        f = _peer_dense(h2f, p["peer_u_t"], p["peer_v"], w, tb, eb, 1024)
        y = _final_ln(x1, f.reshape(b, s, d), mod, p["ln2_g"], p["ln2_b"], alpha, ts)
    return y, k, v, conv_state


def kernel(x_prompt, x_sample, cache_k, cache_v, state_conv, page_table, c_prompt, c_sample, rel_bias,
           w_ada, b_ada, w_in, lambda_q1, lambda_k1, lambda_q2, lambda_k2, subln_w, conv_w, w_out,
           ln1_g, ln1_b, w_pq, sub_keys, peer_u, peer_v, ln2_g, ln2_b):
    depth = w_ada.shape[0]
    bp, sp, d = x_prompt.shape
    bs, tsmp, _ = x_sample.shape
    alpha = (2.0 * depth) ** 0.25
    y_p, y_s = x_prompt, x_sample
    outs = [[] for _ in range(6)]
    zero_prefix = jnp.zeros((bp, CONV_K - 1, conv_w.shape[2]), x_prompt.dtype)
    c_all = jnp.concatenate([c_prompt, c_sample], axis=0)
    n_pool, page = cache_k.shape[1], cache_k.shape[2]
    ck_all = jnp.transpose(cache_k, (0, 1, 3, 4, 5, 2)).reshape(depth * n_pool, N_HEADS, 2, HEAD_DIM, page)
    cv_all = cache_v.reshape(depth * n_pool * page * N_HEADS, V_DIM)
    row = lambda a: a.reshape(1, -1)
    for layer in range(depth):
        lam_init = 0.8 - 0.6 * math.exp(-0.3 * layer)
        p = dict(w_in=w_in[layer].astype(BF16), conv_w=conv_w[layer], w_out=w_out[layer].astype(BF16),
                 ln1_g=row(ln1_g[layer]), ln1_b=row(ln1_b[layer]),
                 w_pq_t=w_pq[layer].T.astype(BF16),
                 sub_keys=sub_keys[layer].reshape(2 * PEER_HEADS, PEER_N_KEYS, -1),
                 peer_u_t=peer_u[layer].T.astype(BF16), peer_v=peer_v[layer].astype(BF16),
                 ln2_g=row(ln2_g[layer]), ln2_b=row(ln2_b[layer]))
        lam_args = (row(lambda_q1[layer]), row(lambda_k1[layer]), row(lambda_q2[layer]),
                    row(lambda_k2[layer]), row(subln_w[layer]))
        mod = _adaln(c_all, w_ada[layer], b_ada[layer])[:, None, :]
        attend_p = lambda q, k, v: _prompt_attention(q, k, v, rel_bias, *lam_args, lam_init, 256, 2)
        pt_layer = page_table + layer * n_pool
        attend_s = lambda q, k, v: _sample_attention(q, k, v, ck_all, cv_all, pt_layer,
                                                     rel_bias, *lam_args, lam_init,
                                                     _pick(page_table.shape[1], (16, 8, 4, 2)))
        y_p, kp, vp, cp = _decoder_layer(y_p, mod[:bp], zero_prefix, attend_p, p, alpha)
        y_s, ks, vs, cs = _decoder_layer(y_s, mod[bp:], state_conv[layer], attend_s, p, alpha)
        for lst, val in zip(outs, (kp.reshape(bp, sp, N_HEADS, 2, HEAD_DIM), vp.reshape(bp, sp, N_HEADS, V_DIM), cp,
                                   ks.reshape(bs, tsmp, N_HEADS, 2, HEAD_DIM), vs.reshape(bs, tsmp, N_HEADS, V_DIM), cs)):
            lst.append(val)
    return (y_p, y_s) + tuple(jnp.stack(o) for o in outs)
```

```python
import functools
import math

import numpy as np
import jax
import jax.numpy as jnp
from jax import lax
from jax.experimental import pallas as pl
from jax.experimental.pallas import tpu as pltpu

N_HEADS = 4
HEAD_DIM = 64
V_DIM = 2 * HEAD_DIM
ATT_WIDTH = N_HEADS * V_DIM
QK_WIDTH = N_HEADS * 2 * HEAD_DIM
CONV_K = 3
NUM_BUCKETS = 32
MAX_DISTANCE = 128
PEER_HEADS = 8
PEER_N_KEYS = 128
PEER_TOPK = 16
LN_EPS = 1e-5

LANES = 128
SUBLANES = 8
VMEM_LIMIT_BYTES = 56 * 1024 * 1024
DENSE_LOOKAHEAD = 1
DENSE_KSLICE = 256

NEG = -1e30
BF16 = jnp.bfloat16
F32 = jnp.float32


def _cparams(sem):
    return pltpu.CompilerParams(dimension_semantics=sem, vmem_limit_bytes=VMEM_LIMIT_BYTES)


def _dot(a, b):
    return jnp.dot(a, b, preferred_element_type=F32)


def _dot_nt(a, b):
    return lax.dot_general(a, b, (((1,), (1,)), ((), ())), preferred_element_type=F32)


def _adaln_kernel(c_ref, w_ref, b_ref, o_ref):
    c = c_ref[...]
    s = c * jax.nn.sigmoid(c)
    o_ref[...] = _dot(s.astype(BF16), w_ref[...].astype(BF16)) + b_ref[...]


def _adaln(c, w_ada, b_ada):
    n, d = c.shape
    width = w_ada.shape[1]
    tn = width // 4
    return pl.pallas_call(
        _adaln_kernel,
        grid=(width // tn,),
        in_specs=[pl.BlockSpec((n, d), lambda j: (0, 0)),
                  pl.BlockSpec((d, tn), lambda j: (0, j)),
                  pl.BlockSpec((1, tn), lambda j: (0, j))],
        out_specs=pl.BlockSpec((n, tn), lambda j: (0, j)),
        out_shape=jax.ShapeDtypeStruct((n, width), F32),
        compiler_params=_cparams(("arbitrary",)),
        name="adaln",
    )(c, w_ada, b_ada.reshape(1, width))


def _inproj_kernel(x_ref, mod_ref, pre_ref, w_ref, cw_ref, q_ref, k_ref, v_ref, oc_ref, cs_ref, zbuf,
                   *, ts, d, cw):
    @pl.when(pl.program_id(1) == 0)
    def _():
        zbuf[SUBLANES - 2:SUBLANES, :] = pre_ref[0]

    shift1 = mod_ref[0, :, 0:d]
    scale1 = mod_ref[0, :, d:2 * d]
    h = x_ref[0] * (1.0 + scale1) + shift1
    proj = _dot(h.astype(BF16), w_ref[...])
    o1 = 2 * QK_WIDTH
    o2 = o1 + ATT_WIDTH
    q_ref[0] = proj[:, :QK_WIDTH]
    k_ref[0] = proj[:, QK_WIDTH:o1]
    v_ref[0] = proj[:, o1:o2]
    gb = proj[:, o2:o2 + cw]
    z = proj[:, o2 + cw:o2 + 2 * cw] * proj[:, o2 + 2 * cw:o2 + 3 * cw]
    zbuf[SUBLANES:SUBLANES + ts, :] = z
    y = (cw_ref[0:1, :] * zbuf[SUBLANES - 2:SUBLANES - 2 + ts, :]
         + cw_ref[1:2, :] * zbuf[SUBLANES - 1:SUBLANES - 1 + ts, :]
         + cw_ref[2:3, :] * z)
    oc_ref[0] = gb * y
    tail = zbuf[SUBLANES + ts - 2:SUBLANES + ts, :]
    cs_ref[0] = tail
    zbuf[SUBLANES - 2:SUBLANES, :] = tail


def _inproj(x, mod, prefix, w_in_bf, conv_w, ts):
    b, s, d = x.shape
    cw = conv_w.shape[1]
    pw = w_in_bf.shape[1]
    blk = lambda width: pl.BlockSpec((1, ts, width), lambda i, j: (i, j, 0))
    outs = pl.pallas_call(
        functools.partial(_inproj_kernel, ts=ts, d=d, cw=cw),
        grid=(b, s // ts),
        in_specs=[blk(d),
                  pl.BlockSpec((1, 1, mod.shape[2]), lambda i, j: (i, 0, 0)),
                  pl.BlockSpec((1, CONV_K - 1, cw), lambda i, j: (i, 0, 0)),
                  pl.BlockSpec((d, pw), lambda i, j: (0, 0)),
                  pl.BlockSpec((CONV_K, cw), lambda i, j: (0, 0))],
        out_specs=[blk(QK_WIDTH), blk(QK_WIDTH), blk(ATT_WIDTH), blk(cw),
                   pl.BlockSpec((1, CONV_K - 1, cw), lambda i, j: (i, 0, 0))],
        out_shape=[jax.ShapeDtypeStruct((b, s, QK_WIDTH), F32),
                   jax.ShapeDtypeStruct((b, s, QK_WIDTH), F32),
                   jax.ShapeDtypeStruct((b, s, ATT_WIDTH), F32),
                   jax.ShapeDtypeStruct((b, s, cw), F32),
                   jax.ShapeDtypeStruct((b, CONV_K - 1, cw), F32)],
        scratch_shapes=[pltpu.VMEM((SUBLANES + ts, cw), F32)],
        compiler_params=_cparams(("arbitrary", "arbitrary")),
        name="inproj",
    )(x, mod, prefix, w_in_bf, conv_w)
    return outs


def _t5_bucket_np(dist):
    dist = np.asarray(dist, np.int64)
    n = np.maximum(dist, 0)
    max_exact = NUM_BUCKETS // 2
    n_large = np.maximum(n, max_exact).astype(np.float64)
    large = max_exact + (np.log(n_large / max_exact) / math.log(MAX_DISTANCE / max_exact)
                         * (NUM_BUCKETS - max_exact)).astype(np.int64)
    large = np.minimum(large, NUM_BUCKETS - 1)
    bucket = np.where(n < max_exact, n, large)
    return np.where(dist < 0, -1, bucket).astype(np.int32)


def _bias_kernel(rb_ref, bk_ref, o_ref, *, rel_to):
    h = pl.program_id(0)
    bk = bk_ref[0]
    base = 0.0 if rel_to is None else rb_ref[rel_to, h]
    acc = jnp.where(bk < 0, NEG, 0.0).astype(F32)
    for b in range(NUM_BUCKETS):
        acc = jnp.where(bk == b, rb_ref[b, h] - base, acc)
    o_ref[0, 0] = acc


def _bias_tiles(rel_bias, buckets, rel_to=None):
    n, r, c = buckets.shape
    return pl.pallas_call(
        functools.partial(_bias_kernel, rel_to=rel_to),
        grid=(N_HEADS, n),
        in_specs=[pl.BlockSpec(memory_space=pltpu.SMEM),
                  pl.BlockSpec((1, r, c), lambda h, i: (i, 0, 0))],
        out_specs=pl.BlockSpec((1, 1, r, c), lambda h, i: (h, i, 0, 0)),
        out_shape=jax.ShapeDtypeStruct((N_HEADS, n, r, c), F32),
        compiler_params=_cparams(("arbitrary", "arbitrary")),
        name="bias_tiles",
    )(rel_bias, jnp.asarray(buckets))


def _far_bucket_from(dist_lo):
    b = _t5_bucket_np(np.arange(dist_lo, dist_lo + 4 * MAX_DISTANCE))
    assert (b == NUM_BUCKETS - 1).all()
    return NUM_BUCKETS - 1


def _diff_lambda(lq1_ref, lk1_ref, lq2_ref, lk2_ref, lam_init):
    a = jnp.sum(lq1_ref[...] * lk1_ref[...], axis=1, keepdims=True)
    b = jnp.sum(lq2_ref[...] * lk2_ref[...], axis=1, keepdims=True)
    return jnp.exp(a) - jnp.exp(b) + lam_init


def _split_q(q):
    lane = lax.broadcasted_iota(jnp.int32, q.shape, 1)
    q1 = jnp.where(lane < HEAD_DIM, q, 0.0)
    q2 = jnp.where(lane >= HEAD_DIM, q, 0.0)
    return jnp.concatenate([q1, q2], axis=0).astype(BF16)


def _online_update(s, v_bf, m_ref, l_ref, acc_ref, rows):
    m_old = m_ref[rows]
    m_new = jnp.maximum(m_old, jnp.max(s, axis=1, keepdims=True))
    p = jnp.exp(s - m_new)
    alpha = jnp.exp(m_old - m_new)
    l_ref[rows] = alpha * l_ref[rows] + jnp.sum(p, axis=1, keepdims=True)
    acc_ref[rows] = alpha * acc_ref[rows] + _dot(p.astype(BF16), v_bf)
    m_ref[rows] = m_new


def _diff_finish(acc, l, lam, sw, nq, lam_init):
    o = acc[:nq] / l[:nq] - lam * (acc[nq:] / l[nq:])
    o = o * lax.rsqrt(jnp.mean(o * o, axis=1, keepdims=True) + LN_EPS)
    return o * sw * (1.0 - lam_init)


def _pattn_kernel(q_ref, k_ref, v_ref, bt_ref, lq1, lk1, lq2, lk2, sw_ref, o_ref,
                  kb_ref, vt_ref, m_ref, l_ref, acc_ref, *, qb, nh, lam_init):
    i = pl.program_id(2)
    n_kt = kb_ref.shape[1]

    @pl.when(i == 0)
    def _():
        for g in range(nh):
            cols = slice(g * V_DIM, (g + 1) * V_DIM)
            for c in range(n_kt):
                kb_ref[g, c] = k_ref[0, c * qb:(c + 1) * qb, cols].astype(BF16)
                vt_ref[g, c] = v_ref[0, c * qb:(c + 1) * qb, cols].T.astype(BF16)

    q2 = [_split_q(q_ref[0, :, g * V_DIM:(g + 1) * V_DIM] * (HEAD_DIM ** -0.5)) for g in range(nh)]
    m_ref[...] = jnp.full(m_ref.shape, NEG, F32)
    l_ref[...] = jnp.zeros(l_ref.shape, F32)
    acc_ref[...] = jnp.zeros(acc_ref.shape, F32)

    def tiles(items):
        old = [(m_ref[g], l_ref[g], acc_ref[g]) for g in range(nh)]
        new = []
        for g in range(nh):
            sts = []
            for j, slot in items:
                st = _dot_nt(kb_ref[g, j], q2[g])
                sts.append(st if slot is None else st + bt_ref[g, slot])
            m_old, l_old, acc_old = old[g]
            m_new = m_old
            for st in sts:
                m_new = jnp.maximum(m_new, jnp.max(st, axis=0, keepdims=True))
            alpha = jnp.exp(m_old - m_new)
            l_new = alpha * l_old
            acc_new = alpha * acc_old
            for (j, _), st in zip(items, sts):
                p = jnp.exp(st - m_new)
                l_new = l_new + jnp.sum(p, axis=0, keepdims=True)
                acc_new = acc_new + _dot(vt_ref[g, j], p.astype(BF16))
            new.append((m_new, l_new, acc_new))
        for g in range(nh):
            m_ref[g], l_ref[g], acc_ref[g] = new[g]

    n_far = jnp.maximum(i - 1, 0)

    n_quad = n_far // 4

    def far_quad(jj, carry):
        tiles([(4 * jj + r, None) for r in range(4)])
        return carry

    lax.fori_loop(0, n_quad, far_quad, 0)

    for rest in range(4):
        @pl.when((i >= 1) & (n_far % 4 == rest))
        def _(rest=rest):
            tiles([(4 * n_quad + r, None) for r in range(rest)] + [(i - 1, 1), (i, 0)])

    @pl.when(i == 0)
    def _():
        tiles([(i, 0)])

    lam = _diff_lambda(lq1, lk1, lq2, lk2, lam_init)
    for g in range(nh):
        acc = acc_ref[g]
        l = l_ref[g]
        ot = acc[:, :qb] / l[:, :qb] - lam * (acc[:, qb:] / l[:, qb:])
        ot = ot * lax.rsqrt(jnp.mean(ot * ot, axis=0, keepdims=True) + LN_EPS)
        o_ref[0, :, g * V_DIM:(g + 1) * V_DIM] = ot.T * sw_ref[...] * (1.0 - lam_init)


def _prompt_attention(q, k, v, rel_bias, lq1, lk1, lq2, lk2, subln_w, lam_init, qb, nh):
    b, s, _ = q.shape
    kk = np.arange(qb)[:, None]
    qq = np.tile(np.arange(qb), 2)[None, :]
    buckets = np.stack([_t5_bucket_np(qq - kk), _t5_bucket_np(qb + qq - kk)])
    far_bucket = _far_bucket_from(qb + 1)
    bt = _bias_tiles(rel_bias, buckets, rel_to=far_bucket)
    vec = lambda n: pl.BlockSpec((1, n), lambda bi, h, i: (0, 0))
    gw = nh * V_DIM
    n_kt = s // qb
    return pl.pallas_call(
        functools.partial(_pattn_kernel, qb=qb, nh=nh, lam_init=lam_init),
        grid=(b, N_HEADS // nh, n_kt),
        in_specs=[pl.BlockSpec((1, qb, gw), lambda bi, h, i: (bi, i, h)),
                  pl.BlockSpec((1, s, gw), lambda bi, h, i: (bi, 0, h)),
                  pl.BlockSpec((1, s, gw), lambda bi, h, i: (bi, 0, h)),
                  pl.BlockSpec((nh, 2, qb, 2 * qb), lambda bi, h, i: (h, 0, 0, 0)),
                  vec(HEAD_DIM), vec(HEAD_DIM), vec(HEAD_DIM), vec(HEAD_DIM), vec(V_DIM)],
        out_specs=pl.BlockSpec((1, qb, gw), lambda bi, h, i: (bi, i, h)),
        out_shape=jax.ShapeDtypeStruct((b, s, ATT_WIDTH), F32),
        scratch_shapes=[pltpu.VMEM((nh, n_kt, qb, V_DIM), BF16), pltpu.VMEM((nh, n_kt, V_DIM, qb), BF16),
                        pltpu.VMEM((nh, 1, 2 * qb), F32), pltpu.VMEM((nh, 1, 2 * qb), F32),
                        pltpu.VMEM((nh, V_DIM, 2 * qb), F32)],
        compiler_params=_cparams(("arbitrary", "arbitrary", "arbitrary")),
        name="prompt_attention",
    )(q, k, v, bt, lq1, lk1, lq2, lk2, subln_w)


def _sattn_kernel(pt_ref, q_ref, kn_ref, vn_ref, bt_ref, bn_ref, lq1, lk1, lq2, lk2, sw_ref, *rest,
                  pps, t, lam_init):
    kp = rest[:pps]
    vp = rest[pps:2 * pps]
    o_ref = rest[2 * pps]
    m_ref, l_ref, acc_ref = rest[2 * pps + 1:]
    j = pl.program_id(1)
    nj = pl.num_programs(1)
    nq = 2 * t

    @pl.when(j == 0)
    def _():
        m_ref[...] = jnp.full(m_ref.shape, NEG, F32)
        l_ref[...] = jnp.zeros(l_ref.shape, F32)
        acc_ref[...] = jnp.zeros(acc_ref.shape, F32)

    q = q_ref[0] * (HEAD_DIM ** -0.5)
    m_all, l_all, acc_all = m_ref[...], l_ref[...], acc_ref[...]
    new_m, new_l, new_acc = [], [], []
    for h in range(N_HEADS):
        cols = slice(h * V_DIM, (h + 1) * V_DIM)
        rows = slice(h * nq, (h + 1) * nq)
        q2 = _split_q(q[:, cols])
        s = jnp.concatenate([_dot(q2, kp[r][0, h].reshape(V_DIM, -1).astype(BF16)) for r in range(pps)],
                            axis=1)
        s = s + bt_ref[h, 0]
        m_old = m_all[rows]
        m_new = jnp.maximum(m_old, jnp.max(s, axis=1, keepdims=True))
        p = jnp.exp(s - m_new)
        alpha = jnp.exp(m_old - m_new)
        new_l.append(alpha * l_all[rows] + jnp.sum(p, axis=1, keepdims=True))
        p = p.astype(BF16)
        vrow = pl.ds(h, LANES, stride=N_HEADS)
        pv = _dot(p[:, 0:LANES], vp[0][vrow, :].astype(BF16))
        for r in range(1, pps):
            pv = pv + _dot(p[:, r * LANES:(r + 1) * LANES], vp[r][vrow, :].astype(BF16))
        new_acc.append(alpha * acc_all[rows] + pv)
        new_m.append(m_new)
    m_ref[...] = jnp.concatenate(new_m, axis=0)
    l_ref[...] = jnp.concatenate(new_l, axis=0)
    acc_ref[...] = jnp.concatenate(new_acc, axis=0)

    @pl.when(j == nj - 1)
    def _():
        lam = _diff_lambda(lq1, lk1, lq2, lk2, lam_init)
        for h in range(N_HEADS):
            cols = slice(h * V_DIM, (h + 1) * V_DIM)
            rows = slice(h * nq, (h + 1) * nq)
            q2 = _split_q(q[:, cols])
            s = _dot_nt(q2, kn_ref[0, :, cols].astype(BF16)) + bn_ref[h, 0]
            _online_update(s, vn_ref[0, :, cols].astype(BF16), m_ref, l_ref, acc_ref, rows)
            o_ref[0, :, cols] = _diff_finish(acc_ref[rows], l_ref[rows], lam, sw_ref[...], t, lam_init)


def _sample_attention(q, k_new, v_new, cache_k, cache_v, page_table, rel_bias,
                      lq1, lk1, lq2, lk2, subln_w, lam_init, pps):
    bs, t, _ = q.shape
    ck, cv = cache_k, cache_v
    page = ck.shape[4]
    assert page == LANES
    n_pages = page_table.shape[1]
    past = n_pages * page
    nj = n_pages // pps
    chunk = pps * page
    tq = np.tile(np.arange(t), 2)[:, None]
    kc = np.arange(chunk)[None, :]
    last = _t5_bucket_np(past + tq - (past - chunk + kc))
    assert (_t5_bucket_np(past - (past - chunk) + 1 + np.arange(4 * MAX_DISTANCE)) == NUM_BUCKETS - 1).all()
    far = np.full_like(last, NUM_BUCKETS - 1)
    kn = np.arange(page)[None, :]
    newb = np.where(kn < t, _t5_bucket_np(tq - kn), -1).astype(np.int32)
    bt = _bias_tiles(rel_bias, np.stack([far, last]))
    bn = _bias_tiles(rel_bias, newb[None])
    knp = jnp.pad(k_new, ((0, 0), (0, page - t), (0, 0)))
    vnp = jnp.pad(v_new, ((0, 0), (0, page - t), (0, 0)))
    vec = lambda n: pl.BlockSpec((1, n), lambda b, j, pt: (0, 0))

    def kpage_spec(r):
        return pl.BlockSpec((1,) + ck.shape[1:], lambda b, j, pt: (pt[b, j * pps + r], 0, 0, 0, 0))

    def vpage_spec(r):
        return pl.BlockSpec((page * N_HEADS, V_DIM), lambda b, j, pt: (pt[b, j * pps + r], 0))

    grid_spec = pltpu.PrefetchScalarGridSpec(
        num_scalar_prefetch=1,
        grid=(bs, nj),
        in_specs=[pl.BlockSpec((1, t, QK_WIDTH), lambda b, j, pt: (b, 0, 0)),
                  pl.BlockSpec((1, page, QK_WIDTH), lambda b, j, pt: (b, 0, 0)),
                  pl.BlockSpec((1, page, ATT_WIDTH), lambda b, j, pt: (b, 0, 0)),
                  pl.BlockSpec((N_HEADS, 1, 2 * t, chunk), lambda b, j, pt: (0, (j == nj - 1).astype(jnp.int32), 0, 0)),
                  pl.BlockSpec((N_HEADS, 1, 2 * t, page), lambda b, j, pt: (0, 0, 0, 0)),
                  vec(HEAD_DIM), vec(HEAD_DIM), vec(HEAD_DIM), vec(HEAD_DIM), vec(V_DIM)]
                 + [kpage_spec(r) for r in range(pps)] + [vpage_spec(r) for r in range(pps)],
        out_specs=pl.BlockSpec((1, t, ATT_WIDTH), lambda b, j, pt: (b, 0, 0)),
        scratch_shapes=[pltpu.VMEM((N_HEADS * 2 * t, 1), F32), pltpu.VMEM((N_HEADS * 2 * t, 1), F32),
                        pltpu.VMEM((N_HEADS * 2 * t, V_DIM), F32)],
    )
    return pl.pallas_call(
        functools.partial(_sattn_kernel, pps=pps, t=t, lam_init=lam_init),
        grid_spec=grid_spec,
        out_shape=jax.ShapeDtypeStruct((bs, t, ATT_WIDTH), F32),
        compiler_params=_cparams(("arbitrary", "arbitrary")),
        name="sample_attention",
    )(page_table, q, knp, vnp, bt, bn, lq1, lk1, lq2, lk2, subln_w, *([ck] * pps), *([cv] * pps))


def _layer_norm(y, g, b):
    mu = jnp.mean(y, axis=1, keepdims=True)
    yc = y - mu
    var = jnp.mean(yc * yc, axis=1, keepdims=True)
    return yc * lax.rsqrt(var + LN_EPS) * g + b


def _outproj_kernel(oa_ref, oc_ref, x_ref, mod_ref, w_ref, g_ref, b_ref, x1_ref, h2_ref, *, d, alpha):
    aw = oa_ref.shape[2]
    mix = _dot(oa_ref[0].astype(BF16), w_ref[0:aw, :]) + _dot(oc_ref[0].astype(BF16), w_ref[aw:, :])
    gate1 = mod_ref[0, :, 2 * d:3 * d]
    shift2 = mod_ref[0, :, 3 * d:4 * d]
    scale2 = mod_ref[0, :, 4 * d:5 * d]
    x1 = _layer_norm(alpha * x_ref[0] + gate1 * mix, g_ref[...], b_ref[...])
    x1_ref[0] = x1
    h2_ref[0] = (x1 * (1.0 + scale2) + shift2).astype(h2_ref.dtype)


def _outproj(oa, oc, x, mod, w_out_bf, g, bb, alpha, ts):
    b, s, d = x.shape
    blk = lambda width: pl.BlockSpec((1, ts, width), lambda i, j: (i, j, 0))
    vec = pl.BlockSpec((1, d), lambda i, j: (0, 0))
    return pl.pallas_call(
        functools.partial(_outproj_kernel, d=d, alpha=alpha),
        grid=(b, s // ts),
        in_specs=[blk(oa.shape[2]), blk(oc.shape[2]), blk(d),
                  pl.BlockSpec((1, 1, mod.shape[2]), lambda i, j: (i, 0, 0)),
                  pl.BlockSpec(w_out_bf.shape, lambda i, j: (0, 0)), vec, vec],
        out_specs=[blk(d), blk(d)],
        out_shape=[jax.ShapeDtypeStruct((b, s, d), F32), jax.ShapeDtypeStruct((b, s, d), BF16)],
        compiler_params=_cparams(("arbitrary", "arbitrary")),
        name="outproj_ln1",
    )(oa, oc, x, mod, w_out_bf, g, bb)


_CAND_BLOCKS = [(i, 16 if i == 0 else 8, PEER_TOPK // (i + 1)) for i in range(8)]


_STACK_DEPTH = 4
_STACK_SORT = ((0, 1), (2, 3), (1, 2), (0, 1), (2, 3), (1, 2))


def _top16_rows(x, n, v_ref, i_ref):
    sub = lax.broadcasted_iota(jnp.int32, (SUBLANES, x.shape[1]), 0)
    n_grp = n // (SUBLANES * _STACK_DEPTH)
    val, key = [], []
    for g in range(n_grp):
        rows = [(g * _STACK_DEPTH + l) * SUBLANES for l in range(_STACK_DEPTH)]
        sv = [x[r0:r0 + SUBLANES] for r0 in rows]
        sk = [sub + r0 for r0 in rows]
        for a, b in _STACK_SORT:
            swap = sv[b] > sv[a]
            sv[a], sv[b] = jnp.where(swap, sv[b], sv[a]), jnp.where(swap, sv[a], sv[b])
            sk[a], sk[b] = jnp.where(swap, sk[b], sk[a]), jnp.where(swap, sk[a], sk[b])
        val.append(sv)
        key.append(sk)
    for r in range(PEER_TOPK):
        top = val[0][0]
        for g in range(1, n_grp):
            top = jnp.maximum(top, val[g][0])
        m = jnp.max(top, axis=0, keepdims=True)
        cand = jnp.where(val[0][0] == m, key[0][0], n)
        for g in range(1, n_grp):
            cand = jnp.minimum(cand, jnp.where(val[g][0] == m, key[g][0], n))
        idx = jnp.min(cand, axis=0, keepdims=True)
        v_ref[r:r + 1, :] = m
        i_ref[r:r + 1, :] = idx
        if r + 1 < PEER_TOPK:
            for g in range(n_grp):
                hit = key[g][0] == idx
                for l in range(_STACK_DEPTH - 1):
                    val[g][l] = jnp.where(hit, val[g][l + 1], val[g][l])
                    key[g][l] = jnp.where(hit, key[g][l + 1], key[g][l])
                val[g][-1] = jnp.where(hit, -jnp.inf, val[g][-1])


def _route_kernel(h_ref, w_ref, sk_ref, e_ref, g_ref, qt_ref, sv_ref, si_ref, cv_ref, ce_ref, et_ref, gt_ref,
                  *, tb):
    nlb = tb // LANES
    qt_ref[...] = _dot_nt(w_ref[...], h_ref[...])
    half = sk_ref.shape[2]

    def stage1(hp, carry):
        row0 = pl.multiple_of(hp * half, half)
        st = _dot(sk_ref[hp].astype(BF16), qt_ref[pl.ds(row0, half), :].astype(BF16))
        for lb in range(nlb):
            _top16_rows(st[:, lb * LANES:(lb + 1) * LANES], PEER_N_KEYS,
                        sv_ref.at[hp, :, lb * LANES:(lb + 1) * LANES],
                        si_ref.at[hp, :, lb * LANES:(lb + 1) * LANES])
        return carry

    lax.fori_loop(0, 2 * PEER_HEADS, stage1, 0)

    sub8 = lax.broadcasted_iota(jnp.int32, (8, LANES), 0)
    sub16 = lax.broadcasted_iota(jnp.int32, (16, LANES), 0)

    def stage2(h, carry):
        for lb in range(nlb):
            ls = slice(lb * LANES, (lb + 1) * LANES)
            sv0 = sv_ref[2 * h, :, ls]
            sv1 = sv_ref[2 * h + 1, :, ls]
            si0 = si_ref[2 * h, :, ls]
            si1 = si_ref[2 * h + 1, :, ls]
            vals, flats = [], []
            for i, rows, cnt in _CAND_BLOCKS:
                sub = sub16 if rows == 16 else sub8
                vals.append(jnp.where(sub < cnt, sv0[i:i + 1] + sv1[0:rows], -jnp.inf))
                flats.append(i * PEER_TOPK + sub)
            vals.append(sv0[8:16] + sv1[0:1])
            flats.append((sub8 + 8) * PEER_TOPK)
            cand = jnp.concatenate(vals, axis=0)
            flat = jnp.concatenate(flats, axis=0)
            big = PEER_TOPK * PEER_TOPK
            for r in range(PEER_TOPK):
                m = jnp.max(cand, axis=0, keepdims=True)
                fl = jnp.min(jnp.where(cand == m, flat, big), axis=0, keepdims=True)
                cv_ref[r:r + 1, :] = m
                ce_ref[r:r + 1, :] = fl
                cand = jnp.where(flat == fl, -jnp.inf, cand)
            cv = cv_ref[...]
            ex = jnp.exp(cv - cv[0:1])
            g = ex / jnp.sum(ex, axis=0, keepdims=True)
            fl = ce_ref[...]
            fi = fl // PEER_TOPK
            fj = fl - fi * PEER_TOPK
            ea = jnp.zeros_like(fl)
            eb = jnp.zeros_like(fl)
            for i in range(PEER_TOPK):
                ea = jnp.where(fi == i, si0[i:i + 1], ea)
                eb = jnp.where(fj == i, si1[i:i + 1], eb)
            row0 = pl.multiple_of(h * PEER_TOPK, PEER_TOPK)
            gt_ref[pl.ds(row0, PEER_TOPK), ls] = g
            et_ref[pl.ds(row0, PEER_TOPK), ls] = ea * PEER_N_KEYS + eb
        return carry

    lax.fori_loop(0, PEER_HEADS, stage2, 0)
    e_ref[...] = et_ref[...].T
    g_ref[...] = gt_ref[...].T


def _peer_route(h2, w_pq_t_bf, sub_keys, tb):
    t, d = h2.shape
    qw = w_pq_t_bf.shape[0]
    hp, n_keys, half = sub_keys.shape
    slots = PEER_HEADS * PEER_TOPK
    return pl.pallas_call(
        functools.partial(_route_kernel, tb=tb),
        grid=(t // tb,),
        in_specs=[pl.BlockSpec((tb, d), lambda i: (i, 0)),
                  pl.BlockSpec((qw, d), lambda i: (0, 0)),
                  pl.BlockSpec((hp, n_keys, half), lambda i: (0, 0, 0))],
        out_specs=[pl.BlockSpec((tb, slots), lambda i: (i, 0)), pl.BlockSpec((tb, slots), lambda i: (i, 0))],
        out_shape=[jax.ShapeDtypeStruct((t, slots), jnp.int32), jax.ShapeDtypeStruct((t, slots), F32)],
        scratch_shapes=[pltpu.VMEM((qw, tb), F32),
                        pltpu.VMEM((hp, PEER_TOPK, tb), F32), pltpu.VMEM((hp, PEER_TOPK, tb), jnp.int32),
                        pltpu.VMEM((PEER_TOPK, LANES), F32), pltpu.VMEM((PEER_TOPK, LANES), jnp.int32),
                        pltpu.VMEM((slots, tb), jnp.int32), pltpu.VMEM((slots, tb), F32)],
        compiler_params=_cparams(("arbitrary",)),
        name="peer_route",
    )(h2, w_pq_t_bf, sub_keys)


def _expand_kernel(e_ref, g_ref, w_ref, *, tb):
    iota = lax.broadcasted_iota(jnp.int32, (PEER_N_KEYS, e_ref.shape[1]), 0)

    def body(t, carry):
        er = e_ref[pl.ds(t, 1), :]
        gr = g_ref[pl.ds(t, 1), :]
        ea = er // PEER_N_KEYS
        eb = er - ea * PEER_N_KEYS
        oa = jnp.where(iota == ea, gr, 0.0).astype(BF16)
        ob = jnp.where(iota == eb, 1.0, 0.0).astype(BF16)
        w_ref[t] = _dot_nt(oa, ob)
        return carry

    lax.fori_loop(0, tb, body, 0, unroll=64)


def _peer_expand(e, g, tb):
    t, slots = e.shape
    return pl.pallas_call(
        functools.partial(_expand_kernel, tb=tb),
        grid=(t // tb,),
        in_specs=[pl.BlockSpec((tb, slots), lambda i: (i, 0)), pl.BlockSpec((tb, slots), lambda i: (i, 0))],
        out_specs=pl.BlockSpec((tb, PEER_N_KEYS, PEER_N_KEYS), lambda i: (i, 0, 0)),
        out_shape=jax.ShapeDtypeStruct((t, PEER_N_KEYS, PEER_N_KEYS), F32),
        compiler_params=_cparams(("arbitrary",)),
        name="peer_expand",
    )(e, g)


def _dense_kernel(h_ref, ut_ref, v_ref, w_ref, *rest, eb, ec, look, d, alpha):
    f_ref = rest[-1]
    j = pl.program_id(1)

    @pl.when(j == 0)
    def _():
        f_ref[...] = jnp.zeros(f_ref.shape, F32)

    hb = h_ref[...]
    nc = eb // ec
    pre = lambda c: _dot(hb, ut_ref[:, c * ec:(c + 1) * ec])
    queue = [pre(c) for c in range(min(look, nc))]
    for c in range(nc):
        a = queue.pop(0)
        if c + look < nc:
            queue.append(pre(c + look))
        acc = None
        for s0 in range(0, ec, DENSE_KSLICE):
            parts = []
            for al in range(s0 // PEER_N_KEYS, (s0 + DENSE_KSLICE) // PEER_N_KEYS):
                ai = c * (ec // PEER_N_KEYS) + al
                x = a[:, al * PEER_N_KEYS:(al + 1) * PEER_N_KEYS]
                x = 0.5 * x * (1.0 + lax.erf(x * (2.0 ** -0.5)))
                parts.append((x * w_ref[:, ai, :]).astype(BF16))
            wa = jnp.concatenate(parts, axis=1)
            part = _dot(wa, v_ref[c * ec + s0:c * ec + s0 + DENSE_KSLICE, :])
            acc = part if acc is None else acc + part
        f_ref[...] += acc

    if len(rest) > 1:
        x1_ref, mod_ref, g_ref, b_ref = rest[:4]

        @pl.when(j == pl.num_programs(1) - 1)
        def _():
            gate2 = mod_ref[0, :, 5 * d:6 * d]
            f_ref[...] = _layer_norm(alpha * x1_ref[...] + gate2 * f_ref[...], g_ref[...], b_ref[...])


def _peer_dense(h2, u_t_bf, v_bf, w, tb, eb, ec, ln=None):
    t, d = h2.shape
    n_exp = v_bf.shape[0]
    in_specs = [pl.BlockSpec((tb, d), lambda i, j: (i, 0)),
                pl.BlockSpec((d, eb), lambda i, j: (0, j)),
                pl.BlockSpec((eb, d), lambda i, j: (j, 0)),
                pl.BlockSpec((tb, eb // PEER_N_KEYS, PEER_N_KEYS), lambda i, j: (i, j, 0))]
    args = [h2, u_t_bf, v_bf, w]
    alpha = 0.0
    if ln is not None:
        x1, mod, g, bb, alpha, per_row = ln
        assert per_row % tb == 0
        blocks_per_row = per_row // tb
        in_specs += [pl.BlockSpec((tb, d), lambda i, j: (i, 0)),
                     pl.BlockSpec((1, 1, mod.shape[2]), lambda i, j: (i // blocks_per_row, 0, 0)),
                     pl.BlockSpec((1, d), lambda i, j: (0, 0)), pl.BlockSpec((1, d), lambda i, j: (0, 0))]
        args += [x1, mod, g, bb]
    return pl.pallas_call(
        functools.partial(_dense_kernel, eb=eb, ec=ec, look=DENSE_LOOKAHEAD, d=d, alpha=alpha),
        grid=(t // tb, n_exp // eb),
        in_specs=in_specs,
        out_specs=pl.BlockSpec((tb, d), lambda i, j: (i, 0)),
        out_shape=jax.ShapeDtypeStruct((t, d), F32),
        compiler_params=_cparams(("arbitrary", "arbitrary")),
        name="peer_dense",
    )(*args)


def _final_kernel(x1_ref, f_ref, mod_ref, g_ref, b_ref, y_ref, *, d, alpha):
    gate2 = mod_ref[0, :, 5 * d:6 * d]
    y_ref[0] = _layer_norm(alpha * x1_ref[0] + gate2 * f_ref[0], g_ref[...], b_ref[...])


def _final_ln(x1, f, mod, g, bb, alpha, ts):
    b, s, d = x1.shape
    blk = pl.BlockSpec((1, ts, d), lambda i, j: (i, j, 0))
    vec = pl.BlockSpec((1, d), lambda i, j: (0, 0))
    return pl.pallas_call(
        functools.partial(_final_kernel, d=d, alpha=alpha),
        grid=(b, s // ts),
        in_specs=[blk, blk, pl.BlockSpec((1, 1, mod.shape[2]), lambda i, j: (i, 0, 0)), vec, vec],
        out_specs=blk,
        out_shape=jax.ShapeDtypeStruct((b, s, d), F32),
        compiler_params=_cparams(("arbitrary", "arbitrary")),
        name="final_ln2",
    )(x1, f, mod, g, bb)


def _pick(n, prefs):
    for p in prefs:
        if n % p == 0:
            return p
    return n


def _decoder_layer(x, mod, prefix, attend, p, alpha):
    b, s, d = x.shape
    ts = _pick(s, (512, 256, 128))
    q, k, v, oc, conv_state = _inproj(x, mod, prefix, p["w_in"], p["conv_w"], ts)
    oa = attend(q, k, v)
    x1, h2 = _outproj(oa, oc, x, mod, p["w_out"], p["ln1_g"], p["ln1_b"], alpha, ts)
    t = b * s
    h2f = h2.reshape(t, d)
    e, g = _peer_route(h2f, p["w_pq_t"], p["sub_keys"], _pick(t, (1024, 512, 256, 128)))
    w = _peer_expand(e, g, _pick(t, (256, 128)))
    n_exp = p["peer_v"].shape[0]
    tb = _pick(t, (512, 256, 128))
    eb = _pick(n_exp, (2048,))
    if s % tb == 0:
        y = _peer_dense(h2f, p["peer_u_t"], p["peer_v"], w, tb, eb, 1024,
                        ln=(x1.reshape(t, d), mod, p["ln2_g"], p["ln2_b"], alpha, s)).reshape(b, s, d)
    else:
        f = _peer_dense(h2f, p["peer_u_t"], p["peer_v"], w, tb, eb, 1024)
        y = _final_ln(x1, f.reshape(b, s, d), mod, p["ln2_g"], p["ln2_b"], alpha, ts)
    return y, k, v, conv_state


def kernel(x_prompt, x_sample, cache_k, cache_v, state_conv, page_table, c_prompt, c_sample, rel_bias,
           w_ada, b_ada, w_in, lambda_q1, lambda_k1, lambda_q2, lambda_k2, subln_w, conv_w, w_out,
           ln1_g, ln1_b, w_pq, sub_keys, peer_u, peer_v, ln2_g, ln2_b):
    depth = w_ada.shape[0]
    bp, sp, d = x_prompt.shape
    bs, tsmp, _ = x_sample.shape
    alpha = (2.0 * depth) ** 0.25
    y_p, y_s = x_prompt, x_sample
    outs = [[] for _ in range(6)]
    zero_prefix = jnp.zeros((bp, CONV_K - 1, conv_w.shape[2]), x_prompt.dtype)
    c_all = jnp.concatenate([c_prompt, c_sample], axis=0)
    n_pool, page = cache_k.shape[1], cache_k.shape[2]
    ck_all = jnp.transpose(cache_k, (0, 1, 3, 4, 5, 2)).reshape(depth * n_pool, N_HEADS, 2, HEAD_DIM, page)
    cv_all = cache_v.reshape(depth * n_pool * page * N_HEADS, V_DIM)
    row = lambda a: a.reshape(1, -1)
    for layer in range(depth):
        lam_init = 0.8 - 0.6 * math.exp(-0.3 * layer)
        p = dict(w_in=w_in[layer].astype(BF16), conv_w=conv_w[layer], w_out=w_out[layer].astype(BF16),
                 ln1_g=row(ln1_g[layer]), ln1_b=row(ln1_b[layer]),
                 w_pq_t=w_pq[layer].T.astype(BF16),
                 sub_keys=sub_keys[layer].reshape(2 * PEER_HEADS, PEER_N_KEYS, -1),
                 peer_u_t=peer_u[layer].T.astype(BF16), peer_v=peer_v[layer].astype(BF16),
                 ln2_g=row(ln2_g[layer]), ln2_b=row(ln2_b[layer]))
        lam_args = (row(lambda_q1[layer]), row(lambda_k1[layer]), row(lambda_q2[layer]),
                    row(lambda_k2[layer]), row(subln_w[layer]))
        mod = _adaln(c_all, w_ada[layer], b_ada[layer])[:, None, :]
        attend_p = lambda q, k, v: _prompt_attention(q, k, v, rel_bias, *lam_args, lam_init, 256, 2)
        pt_layer = page_table + layer * n_pool
        attend_s = lambda q, k, v: _sample_attention(q, k, v, ck_all, cv_all, pt_layer,
                                                     rel_bias, *lam_args, lam_init,
                                                     _pick(page_table.shape[1], (16, 8, 4, 2)))
        y_p, kp, vp, cp = _decoder_layer(y_p, mod[:bp], zero_prefix, attend_p, p, alpha)
        y_s, ks, vs, cs = _decoder_layer(y_s, mod[bp:], state_conv[layer], attend_s, p, alpha)
        for lst, val in zip(outs, (kp.reshape(bp, sp, N_HEADS, 2, HEAD_DIM), vp.reshape(bp, sp, N_HEADS, V_DIM), cp,
                                   ks.reshape(bs, tsmp, N_HEADS, 2, HEAD_DIM), vs.reshape(bs, tsmp, N_HEADS, V_DIM), cs)):
            lst.append(val)
    return (y_p, y_s) + tuple(jnp.stack(o) for o in outs)
```
